```python
import math
import jax, jax.numpy as jnp
from jax import lax
import numpy as np

D_MODEL = 1024
BATCH = 16
SEQ = 2048
DEPTH = 1
DEC_BATCH = 16
DEC_SEQ = 4096
PAST_LEN = 128

D_CONV = D_MODEL // 2
D_SSM = D_MODEL - D_CONV
D_MIX = D_CONV + D_SSM
CONV_HEADS = 8
SSM_GROUP = 16
SSM_GROUPS = D_SSM // SSM_GROUP
SSM_STATE = 64
D_IN = 3 * D_CONV + D_SSM
D_FF = 2816
EPS = 1e-6
DT_MIN = 1e-3
DT_MAX = 1e-1

kernel_name = "hymba_conv_s5_sandwich_encoder"


def rms_norm(x, g):
    xf = x.astype(jnp.float32)
    y = xf * lax.rsqrt(jnp.mean(xf * xf, axis=-1, keepdims=True) + EPS)
    return (y * g.astype(jnp.float32)).astype(x.dtype)


def dwconv3(x, w):
    L = x.shape[1]
    xp = jnp.pad(x, ((0, 0), (1, 1), (0, 0)))
    return xp[:, 0:L] * w[0] + xp[:, 1:L + 1] * w[1] + xp[:, 2:L + 2] * w[2]


def _linear_recurrence_combine(e1, e2):
    a1, b1 = e1
    a2, b2 = e2
    return (a1 * a2, a2 * b1 + b2)


def _s5_direction(u, lam_re, lam_im, log_step, b_re, b_im, c_re, c_im):
    L = u.shape[0]
    f32 = jnp.float32
    lam = lax.complex(lam_re.astype(f32), lam_im.astype(f32))
    dt = jnp.exp(log_step.astype(f32))[:, None]
    lam_bar = jnp.exp(lam * dt)
    bmat = lax.complex(b_re.astype(f32), b_im.astype(f32))
    b_bar = ((lam_bar - 1.0) / lam)[..., None] * bmat
    bu = jnp.einsum('lbgh,gph->lbgp', u.astype(jnp.complex64), b_bar)
    a = jnp.broadcast_to(lam_bar[None, None], (L, 1) + lam_bar.shape)
    _, s = lax.associative_scan(_linear_recurrence_combine, (a, bu), axis=0)
    cmat = lax.complex(c_re.astype(f32), c_im.astype(f32))
    return jnp.real(jnp.einsum('lbgp,ghp->lbgh', s, cmat))


def s5_mixer(u, lam_re, lam_im, log_step, b_re, b_im, c_re, c_im, d_skip, w_glu, b_glu):
    Bt, L, _ = u.shape
    uf = u.astype(jnp.float32)
    ul = uf.reshape(Bt, L, SSM_GROUPS, SSM_GROUP).transpose(1, 0, 2, 3)
    y_fwd = _s5_direction(ul, lam_re[0], lam_im[0], log_step[0], b_re[0], b_im[0], c_re[0], c_im[0])
    y_bwd = jnp.flip(_s5_direction(jnp.flip(ul, 0), lam_re[1], lam_im[1], log_step[1],
                                   b_re[1], b_im[1], c_re[1], c_im[1]), 0)
    y = (y_fwd + y_bwd).transpose(1, 0, 2, 3).reshape(Bt, L, D_SSM) + uf * d_skip.astype(jnp.float32)
    y = jax.nn.gelu(y).astype(u.dtype)
    return y * jax.nn.sigmoid(y @ w_glu + b_glu)


def encoder_layer(x, pre_mix_g, w_in, conv_w, lam_re, lam_im, log_step, b_re, b_im, c_re, c_im,
                  d_skip, w_glu, b_glu, gn_conv, gn_ssm, w_out, post_mix_g,
                  pre_ffn_g, w_up, ffn_conv_w, ffn_conv_b, w_down, post_ffn_g):
    h = rms_norm(x, pre_mix_g)
    z = h @ w_in
    zb = z[..., 0:D_CONV]
    zc = z[..., D_CONV:2 * D_CONV]
    zx = z[..., 2 * D_CONV:3 * D_CONV]
    zu = z[..., 3 * D_CONV:]
    y_conv = zb * dwconv3(zc * zx, conv_w)
    y_ssm = s5_mixer(zu, lam_re, lam_im, log_step, b_re, b_im, c_re, c_im, d_skip, w_glu, b_glu)
    y = jnp.concatenate([rms_norm(y_conv, gn_conv), rms_norm(y_ssm, gn_ssm)], axis=-1) @ w_out
    x = x + rms_norm(y, post_mix_g)
    h = rms_norm(x, pre_ffn_g)
    up = dwconv3(h @ w_up, ffn_conv_w) + ffn_conv_b
    gate = up[..., :D_FF]
    val = up[..., D_FF:]
    f = (jax.nn.silu(gate) * val) @ w_down
    return x + rms_norm(f, post_ffn_g)


def setup_inputs(seed: int = 0) -> dict:
    key = jax.random.key(seed)
    ks = jax.random.split(key, 32)
    f32 = jnp.float32
    nrm = lambda k, shape, s: jax.random.normal(k, shape, f32) * s
    gain = lambda k, n: 1.0 + 0.01 * jax.random.normal(k, (DEPTH, n), f32)
    G, P, H = SSM_GROUPS, SSM_STATE, SSM_GROUP
    n_idx = jnp.arange(P, dtype=f32)
    lam_re = -0.5 + 0.01 * jax.random.normal(ks[2], (DEPTH, 2, G, P), f32)
    lam_im = math.pi * n_idx + 0.01 * jax.random.normal(ks[3], (DEPTH, 2, G, P), f32)
    log_step = jax.random.uniform(ks[4], (DEPTH, 2, G), f32, math.log(DT_MIN), math.log(DT_MAX))
    return {
        "x_prompt": jax.random.normal(ks[0], (BATCH, SEQ, D_MODEL), f32),
        "x_sample": jax.random.normal(ks[1], (DEC_BATCH, DEC_SEQ, D_MODEL), f32),
        "pre_mix_g": gain(ks[5], D_MODEL),
        "w_in": nrm(ks[6], (DEPTH, D_MODEL, D_IN), D_MODEL ** -0.5),
        "conv_w": nrm(ks[7], (DEPTH, 3, D_CONV), 3 ** -0.5),
        "lam_re": lam_re,
        "lam_im": lam_im,
        "log_step": log_step,
        "b_re": nrm(ks[8], (DEPTH, 2, G, P, H), (2 * H) ** -0.5),
        "b_im": nrm(ks[9], (DEPTH, 2, G, P, H), (2 * H) ** -0.5),
        "c_re": nrm(ks[10], (DEPTH, 2, G, H, P), (2 * P) ** -0.5),
        "c_im": nrm(ks[11], (DEPTH, 2, G, H, P), (2 * P) ** -0.5),
        "d_skip": nrm(ks[12], (DEPTH, D_SSM), 1.0),
        "w_glu": nrm(ks[13], (DEPTH, D_SSM, D_SSM), D_SSM ** -0.5),
        "b_glu": nrm(ks[14], (DEPTH, D_SSM), 0.01),
        "gn_conv": gain(ks[15], D_CONV),
        "gn_ssm": gain(ks[16], D_SSM),
        "w_out": nrm(ks[17], (DEPTH, D_MIX, D_MODEL), D_MIX ** -0.5),
        "post_mix_g": gain(ks[18], D_MODEL),
        "pre_ffn_g": gain(ks[19], D_MODEL),
        "w_up": nrm(ks[20], (DEPTH, D_MODEL, 2 * D_FF), D_MODEL ** -0.5),
        "ffn_conv_w": nrm(ks[21], (DEPTH, 3, 2 * D_FF), 3 ** -0.5),
        "ffn_conv_b": nrm(ks[22], (DEPTH, 2 * D_FF), 0.01),
        "w_down": nrm(ks[23], (DEPTH, D_FF, D_MODEL), D_FF ** -0.5),
        "post_ffn_g": gain(ks[24], D_MODEL),
    }


def reference(x_prompt, x_sample, pre_mix_g, w_in, conv_w, lam_re, lam_im, log_step, b_re, b_im,
              c_re, c_im, d_skip, w_glu, b_glu, gn_conv, gn_ssm, w_out, post_mix_g,
              pre_ffn_g, w_up, ffn_conv_w, ffn_conv_b, w_down, post_ffn_g):
    def trunk(x):
        for l in range(DEPTH):
            x = encoder_layer(x, pre_mix_g[l], w_in[l], conv_w[l], lam_re[l], lam_im[l], log_step[l],
                              b_re[l], b_im[l], c_re[l], c_im[l], d_skip[l], w_glu[l], b_glu[l],
                              gn_conv[l], gn_ssm[l], w_out[l], post_mix_g[l],
                              pre_ffn_g[l], w_up[l], ffn_conv_w[l], ffn_conv_b[l], w_down[l], post_ffn_g[l])
        return x
    y_prompt = trunk(x_prompt)
    y_sample = trunk(x_sample)
    return (y_prompt, y_sample)
```

```python
import functools

import jax
import jax.numpy as jnp
from jax import lax
from jax.experimental import pallas as pl
from jax.experimental.pallas import tpu as pltpu

D_MODEL = 1024
D_CONV = 512
D_SSM = 512
SSM_GROUP = 16
SSM_GROUPS = D_SSM // SSM_GROUP
SSM_STATE = 64
D_FF = 2816
EPS = 1e-6
LANES = 128

CHUNK = 16
TILE_CHUNKS = 128
TILE = CHUNK * TILE_CHUNKS
HALO = 16
FFN_CHUNKS = 32
FFN_TILE = CHUNK * FFN_CHUNKS
FF_HALF = D_FF // 2
SCAN_PAD = 8
GLU_COLS = 512
VMEM_LIMIT_V7X = 56 * 1024 * 1024

F32 = jnp.float32
BF16 = jnp.bfloat16


def _rms(x, g):
    return x * lax.rsqrt(jnp.mean(x * x, axis=-1, keepdims=True) + EPS) * g


def _dot(a, b):
    return jnp.dot(a, b, preferred_element_type=F32)


def _const_spec(shape):
    zeros = (0,) * len(shape)
    return pl.BlockSpec(shape, lambda *_: zeros, pipeline_mode=pl.Buffered(1))


def _s5_tables(lam_re, lam_im, log_step, b_re, b_im, c_re, c_im):
    T, G, P, H = CHUNK, SSM_GROUPS, SSM_STATE, SSM_GROUP
    lam = lax.complex(lam_re.astype(F32), lam_im.astype(F32))
    dt = jnp.exp(log_step.astype(F32))[..., None]
    lam_bar = jnp.exp(lam * dt)
    b_bar = ((lam_bar - 1.0) / lam)[..., None] * lax.complex(b_re.astype(F32), b_im.astype(F32))
    cmat = lax.complex(c_re.astype(F32), c_im.astype(F32))
    k = jnp.arange(T + 1, dtype=F32)
    pw = jnp.exp((lam * dt)[..., None] * k)

    kern = jnp.real(jnp.einsum('dgap,dgpk,dgph->dgkah', cmat, pw[..., :T], b_bar))
    tt = jnp.arange(T)
    lag = tt[:, None] - tt[None, :]
    kf = jnp.where((lag >= 0)[None, :, :, None, None], kern[0][:, jnp.clip(lag, 0, T - 1)], 0.0)
    kb = jnp.where((lag <= 0)[None, :, :, None, None], kern[1][:, jnp.clip(-lag, 0, T - 1)], 0.0)
    toe = (kf + kb).transpose(0, 1, 3, 2, 4).reshape(G, T * H, T * H)

    inc_f = pw[0][:, :, T - 1 - tt][..., None] * b_bar[0][:, :, None, :]
    inc_b = pw[1][:, :, tt][..., None] * b_bar[1][:, :, None, :]
    wv = jnp.concatenate([jnp.real(inc_f), jnp.imag(inc_f), jnp.real(inc_b), jnp.imag(inc_b)],
                         axis=1).reshape(G, 4 * P, T * H)

    st_f = cmat[0][:, None] * pw[0][:, :, 1 + tt].transpose(0, 2, 1)[:, :, None, :]
    st_b = cmat[1][:, None] * pw[1][:, :, T - tt].transpose(0, 2, 1)[:, :, None, :]
    mst = jnp.concatenate([jnp.real(st_f), -jnp.imag(st_f), jnp.real(st_b), -jnp.imag(st_b)],
                          axis=-1).reshape(G, T * H, 4 * P)
    wy = jnp.concatenate([toe, mst], axis=-1)

    a16 = pw[..., T].reshape(2, G // 2, 2 * P)
    coef = jnp.stack([jnp.real(a16[0]), jnp.imag(a16[0]), jnp.real(a16[1]), jnp.imag(a16[1])])
    return wv.astype(BF16), wy.astype(BF16), coef.astype(F32)


def _permuted_slab(xcols, t, n_chunks):
    return jnp.concatenate([xc[0, pl.ds(t, n_chunks, stride=CHUNK), :] for xc in xcols], axis=1)


def _column_block_specs(rows, index_map):
    return [pl.BlockSpec((1, rows, LANES), functools.partial(index_map, cb=cb))
            for cb in range(D_MODEL // LANES)]


def _mixer_in_kernel(*refs):
    ncb = D_MODEL // LANES
    xcols = refs[:ncb]
    xp_ref, xn_ref, g_ref, win_ref, cw_ref, gn_ref, nconv_ref, zut_ref, hn_scr = refs[ncb:]
    i = pl.program_id(1)
    last = pl.num_programs(1) - 1
    T, NC = CHUNK, TILE_CHUNKS
    g = g_ref[...]
    for t in range(T):
        xs = _permuted_slab(xcols, t, NC)
        hn_scr[t * NC:(t + 1) * NC, :] = _rms(xs, g).astype(BF16)
    xh = jnp.concatenate([xp_ref[0], xn_ref[0]], axis=0)
    row = lax.broadcasted_iota(jnp.int32, (HALO, 1), 0)
    inside = ((row < 8) & (i > 0)) | ((row >= 8) & (i < last))
    hn_scr[TILE:TILE + HALO, :] = jnp.where(inside, _rms(xh, g), 0.0).astype(BF16)

    z12 = _dot(hn_scr[...], win_ref[:, D_CONV:3 * D_CONV])
    p = z12[:, :D_CONV] * z12[:, D_CONV:]
    zb = _dot(hn_scr[0:TILE, :], win_ref[:, 0:D_CONV])
    w0, w1, w2 = cw_ref[0:1, :], cw_ref[1:2, :], cw_ref[2:3, :]
    gn = gn_ref[...]
    rid = lax.broadcasted_iota(jnp.int32, (NC, 1), 0)
    for t in range(T):
        cur = p[t * NC:(t + 1) * NC]
        if t > 0:
            prev = p[(t - 1) * NC:t * NC]
        else:
            prev = jnp.where(rid == 0, p[TILE + 7:TILE + 8],
                             pltpu.roll(p[(T - 1) * NC:T * NC], 1, 0))
        if t < T - 1:
            nxt = p[(t + 1) * NC:(t + 2) * NC]
        else:
            nxt = jnp.where(rid == NC - 1, p[TILE + 8:TILE + 9],
                            pltpu.roll(p[0:NC], NC - 1, 0))
        yc = zb[t * NC:(t + 1) * NC] * (w0 * prev + w1 * cur + w2 * nxt)
        nconv_ref[0, 0, t * NC:(t + 1) * NC, :] = _rms(yc, gn).astype(BF16)

    zu = _dot(hn_scr[0:TILE, :], win_ref[:, 3 * D_CONV:])
    zut_ref[0, 0] = zu.T.astype(BF16)


def _mixer_in(x, pre_mix_g, w_in, conv_w, gn_conv):
    B, L, D = x.shape
    nt = L // TILE
    blk8 = TILE // 8
    return pl.pallas_call(
        _mixer_in_kernel,
        grid=(B, nt),
        in_specs=_column_block_specs(TILE, lambda b, i, cb: (b, i, cb)) + [
            pl.BlockSpec((1, 8, D), lambda b, i: (b, jnp.maximum(i * blk8 - 1, 0), 0)),
            pl.BlockSpec((1, 8, D), lambda b, i: (b, jnp.minimum((i + 1) * blk8, L // 8 - 1), 0)),
            _const_spec((1, D)),
            _const_spec((D, 4 * D_CONV)),
            _const_spec((3, D_CONV)),
            _const_spec((1, D_CONV)),
        ],
        out_specs=[
            pl.BlockSpec((1, 1, TILE, D_CONV), lambda b, i: (b, i, 0, 0)),
            pl.BlockSpec((1, 1, D_SSM, TILE), lambda b, i: (b, i, 0, 0)),
        ],
        out_shape=[
            jax.ShapeDtypeStruct((B, nt, TILE, D_CONV), BF16),
            jax.ShapeDtypeStruct((B, nt, D_SSM, TILE), BF16),
        ],
        scratch_shapes=[pltpu.VMEM((TILE + HALO, D), BF16)],
        compiler_params=pltpu.CompilerParams(
            dimension_semantics=("parallel", "parallel"),
            vmem_limit_bytes=VMEM_LIMIT_V7X),
        name="mixer_in",
    )(*([x] * (D // LANES)), x, x, pre_mix_g, w_in, conv_w, gn_conv)


def _s5_kernel(zut_ref, wv_ref, wy_ref, coef_ref, d_ref, wglu_ref, bglu_ref, gn_ref,
               out_ref, vfr, vfi, vbr, vbi, yt_scr, *, nt):
    T, NC, H, P = CHUNK, TILE_CHUNKS, SSM_GROUP, SSM_STATE
    npair = SSM_GROUPS // 2
    nc = nt * NC
    pitch = nc + SCAN_PAD

    def chunk_inputs(i, grp):
        r0 = pl.multiple_of(grp * H, H)
        return jnp.concatenate(
            [zut_ref[0, i, pl.ds(r0, H), t * NC:(t + 1) * NC] for t in range(T)], axis=0)

    for i in range(nt):
        def inc_body(q, carry, i=i):
            va = _dot(wv_ref[2 * q], chunk_inputs(i, 2 * q))
            vb = _dot(wv_ref[2 * q + 1], chunk_inputs(i, 2 * q + 1))
            pieces = []
            for kind in range(4):
                pieces += [va[kind * P:(kind + 1) * P], vb[kind * P:(kind + 1) * P]]
            v = jnp.concatenate(pieces, axis=0).T
            r0 = pl.multiple_of(q * pitch + i * NC, 8)
            vfr[pl.ds(r0, NC), :] = v[:, 0:2 * P]
            vfi[pl.ds(r0, NC), :] = v[:, 2 * P:4 * P]
            vbr[pl.ds(r0, NC), :] = v[:, 4 * P:6 * P]
            vbi[pl.ds(r0, NC), :] = v[:, 6 * P:8 * P]
            return carry
        lax.fori_loop(0, npair, inc_body, 0)

    afr, afi, abr, abi = coef_ref[0], coef_ref[1], coef_ref[2], coef_ref[3]

    def scan_body(k, carry):
        sfr, sfi, sbr, sbi = carry
        rows_f = pl.ds(k, npair, stride=pitch)
        rows_b = pl.ds(nc - 1 - k, npair, stride=pitch)
        ur, ui = vfr[rows_f, :], vfi[rows_f, :]
        wr, wi = vbr[rows_b, :], vbi[rows_b, :]
        vfr[rows_f, :] = sfr
        vfi[rows_f, :] = sfi
        vbr[rows_b, :] = sbr
        vbi[rows_b, :] = sbi
        return (afr * sfr - afi * sfi + ur, afr * sfi + afi * sfr + ui,
                abr * sbr - abi * sbi + wr, abr * sbi + abi * sbr + wi)

    zero = jnp.zeros((npair, 2 * P), F32)
    lax.fori_loop(0, nc, scan_body, (zero, zero, zero, zero))

    for i in range(nt):
        def out_body(q, carry, i=i):
            r0 = pl.multiple_of(q * pitch + i * NC, 8)
            rows = pl.ds(r0, NC)
            sin = jnp.concatenate([vfr[rows, :], vfi[rows, :], vbr[rows, :], vbi[rows, :]],
                                  axis=1).T
            for j in range(2):
                grp = 2 * q + j
                st = jnp.concatenate([sin[(2 * kind + j) * P:(2 * kind + j + 1) * P]
                                      for kind in range(4)], axis=0)
                rhs = jnp.concatenate([chunk_inputs(i, grp), st.astype(BF16)], axis=0)
                yt = _dot(wy_ref[grp], rhs)
                h0 = pl.multiple_of(grp * H, H)
                for t in range(T):
                    yt_scr[pl.ds(h0, H), t * NC:(t + 1) * NC] = yt[t * H:(t + 1) * H]
            return carry
        lax.fori_loop(0, npair, out_body, 0)

        for cb in range(TILE // GLU_COLS):
            cols = slice(cb * GLU_COLS, (cb + 1) * GLU_COLS)
            y = yt_scr[:, cols] + d_ref[...] * zut_ref[0, i, :, cols].astype(F32)
            y = jax.nn.gelu(y)
            gate = _dot(wglu_ref[...], y.astype(BF16)) + bglu_ref[...]
            y = y * jax.nn.sigmoid(gate)
            ms = jnp.mean(y * y, axis=0, keepdims=True)
            y = y * lax.rsqrt(ms + EPS) * gn_ref[...]
            out_ref[0, i, cols, :] = y.T.astype(BF16)


def _s5_mixer(zut, wv, wy, coef, d_col, wglu_t, bglu_col, gn_col):
    B, nt = zut.shape[0], zut.shape[1]
    npair = SSM_GROUPS // 2
    scan_rows = npair * (nt * TILE_CHUNKS + SCAN_PAD)
    return pl.pallas_call(
        functools.partial(_s5_kernel, nt=nt),
        grid=(B,),
        in_specs=[
            pl.BlockSpec((1, nt, D_SSM, TILE), lambda b: (b, 0, 0, 0)),
            _const_spec(wv.shape),
            _const_spec(wy.shape),
            _const_spec(coef.shape),
            _const_spec((D_SSM, 1)),
            _const_spec((D_SSM, D_SSM)),
            _const_spec((D_SSM, 1)),
            _const_spec((D_SSM, 1)),
        ],
        out_specs=pl.BlockSpec((1, nt, TILE, D_SSM), lambda b: (b, 0, 0, 0)),
        out_shape=jax.ShapeDtypeStruct((B, nt, TILE, D_SSM), BF16),
        scratch_shapes=[pltpu.VMEM((scan_rows, 2 * SSM_STATE), F32) for _ in range(4)]
        + [pltpu.VMEM((D_SSM, TILE), F32)],
        compiler_params=pltpu.CompilerParams(
            dimension_semantics=("parallel",),
            vmem_limit_bytes=VMEM_LIMIT_V7X),
        name="s5_mixer",
    )(zut, wv, wy, coef, d_col, wglu_t, bglu_col, gn_col)


def _out_ffn_kernel(*refs):
    ncb = D_MODEL // LANES
    xcols = refs[:ncb]
    (xp_ref, xn_ref, nc_ref, ncp_ref, ncn_ref, ns_ref, nsp_ref, nsn_ref,
     wout_ref, pmg_ref, pfg_ref, wup_ref, fcw_ref, fcb_ref, wdown_ref, pog_ref,
     o_ref, up_scr, act_scr, o_scr) = refs[ncb:]
    j = pl.program_id(1)
    last = pl.num_programs(1) - 1
    T, NC, R = CHUNK, FFN_CHUNKS, FFN_TILE

    def with_halo(main_ref, prev_ref, next_ref):
        main = main_ref[0, 0].reshape(R, main_ref.shape[-1])
        halo = jnp.concatenate([prev_ref[0, 0, 0].astype(F32)[8:16],
                                next_ref[0, 0, 0].astype(F32)[0:8]], axis=0).astype(BF16)
        return jnp.concatenate([main, halo], axis=0)

    x_all = jnp.concatenate([_permuted_slab(xcols, t, NC) for t in range(T)]
                            + [xp_ref[0], xn_ref[0]], axis=0)
    mix = (_dot(with_halo(nc_ref, ncp_ref, ncn_ref), wout_ref[0:D_CONV, :])
           + _dot(with_halo(ns_ref, nsp_ref, nsn_ref), wout_ref[D_CONV:, :]))
    x1 = x_all + _rms(mix, pmg_ref[...])
    row = lax.broadcasted_iota(jnp.int32, (R + HALO, 1), 0)
    inside = (row < R) | ((row < R + 8) & (j > 0)) | ((row >= R + 8) & (j < last))
    h2 = jnp.where(inside, _rms(x1, pfg_ref[...]), 0.0).astype(BF16)

    rid = lax.broadcasted_iota(jnp.int32, (NC, 1), 0)
    for half in range(2):
        gcols = slice(half * FF_HALF, (half + 1) * FF_HALF)
        vcols = slice(D_FF + half * FF_HALF, D_FF + (half + 1) * FF_HALF)
        up_scr[:, 0:FF_HALF] = _dot(h2, wup_ref[:, gcols])
        up_scr[:, FF_HALF:] = _dot(h2, wup_ref[:, vcols])
        w0 = jnp.concatenate([fcw_ref[0:1, gcols], fcw_ref[0:1, vcols]], axis=1)
        w1 = jnp.concatenate([fcw_ref[1:2, gcols], fcw_ref[1:2, vcols]], axis=1)
        w2 = jnp.concatenate([fcw_ref[2:3, gcols], fcw_ref[2:3, vcols]], axis=1)
        bias = jnp.concatenate([fcb_ref[:, gcols], fcb_ref[:, vcols]], axis=1)
        for t in range(T):
            cur = up_scr[t * NC:(t + 1) * NC, :]
            if t > 0:
                prev = up_scr[(t - 1) * NC:t * NC, :]
            else:
                prev = jnp.where(rid == 0, up_scr[R + 7:R + 8, :],
                                 pltpu.roll(up_scr[(T - 1) * NC:T * NC, :], 1, 0))
            if t < T - 1:
                nxt = up_scr[(t + 1) * NC:(t + 2) * NC, :]
            else:
                nxt = jnp.where(rid == NC - 1, up_scr[R + 8:R + 9, :],
                                pltpu.roll(up_scr[0:NC, :], NC - 1, 0))
            u = w0 * prev + w1 * cur + w2 * nxt + bias
            act = jax.nn.silu(u[:, :FF_HALF]) * u[:, FF_HALF:]
            act_scr[t * NC:(t + 1) * NC, gcols] = act.astype(BF16)

    f = _dot(act_scr[...], wdown_ref[...])
    out = x1[0:R] + _rms(f, pog_ref[...])
    for cb in range(ncb):
        for t in range(T):
            o_scr[cb, pl.ds(t, NC, stride=T), :] = out[t * NC:(t + 1) * NC,
                                                       cb * LANES:(cb + 1) * LANES]
        o_ref[0, :, cb * LANES:(cb + 1) * LANES] = o_scr[cb]


def _out_ffn(x, nconv, nssm, w_out, post_mix_g, pre_ffn_g, w_up, ffn_conv_w, ffn_conv_b,
             w_down, post_ffn_g):
    B, L, D = x.shape
    nt = L // TILE
    nj = L // FFN_TILE
    per_tile = TILE_CHUNKS // FFN_CHUNKS
    blk8 = FFN_TILE // 8
    nchunks = L // CHUNK
    nc5 = nconv.reshape(B, nt, CHUNK, TILE_CHUNKS, D_CONV)
    ns5 = nssm.reshape(B, nt, CHUNK, TILE_CHUNKS, D_SSM)

    def main_map(b, j):
        return (b, j // per_tile, 0, j % per_tile, 0)

    def prev_map(b, j):
        c = jnp.maximum(j * FFN_CHUNKS - 1, 0)
        return (b, c // TILE_CHUNKS, CHUNK - 1, (c % TILE_CHUNKS) // 16, 0)

    def next_map(b, j):
        c = jnp.minimum((j + 1) * FFN_CHUNKS, nchunks - 1)
        return (b, c // TILE_CHUNKS, 0, (c % TILE_CHUNKS) // 16, 0)

    act_specs = []
    for width in (D_CONV, D_SSM):
        act_specs += [pl.BlockSpec((1, 1, CHUNK, FFN_CHUNKS, width), main_map),
                      pl.BlockSpec((1, 1, 1, 16, width), prev_map),
                      pl.BlockSpec((1, 1, 1, 16, width), next_map)]
    return pl.pallas_call(
        _out_ffn_kernel,
        grid=(B, nj),
        in_specs=_column_block_specs(FFN_TILE, lambda b, j, cb: (b, j, cb)) + [
            pl.BlockSpec((1, 8, D), lambda b, j: (b, jnp.maximum(j * blk8 - 1, 0), 0)),
            pl.BlockSpec((1, 8, D), lambda b, j: (b, jnp.minimum((j + 1) * blk8, L // 8 - 1), 0)),
        ] + act_specs + [
            _const_spec((D_CONV + D_SSM, D)),
            _const_spec((1, D)),
            _const_spec((1, D)),
            _const_spec((D, 2 * D_FF)),
            _const_spec((3, 2 * D_FF)),
            _const_spec((1, 2 * D_FF)),
            _const_spec((D_FF, D)),
            _const_spec((1, D)),
        ],
        out_specs=pl.BlockSpec((1, FFN_TILE, D), lambda b, j: (b, j, 0)),
        out_shape=jax.ShapeDtypeStruct((B, L, D), F32),
        scratch_shapes=[pltpu.VMEM((FFN_TILE + HALO, 2 * FF_HALF), F32),
                        pltpu.VMEM((FFN_TILE, D_FF), BF16),
                        pltpu.VMEM((D // LANES, FFN_TILE, LANES), F32)],
        compiler_params=pltpu.CompilerParams(
            dimension_semantics=("parallel", "parallel"),
            vmem_limit_bytes=VMEM_LIMIT_V7X),
        name="out_ffn",
    )(*([x] * (D // LANES)), x, x, nc5, nc5, nc5, ns5, ns5, ns5, w_out, post_mix_g, pre_ffn_g, w_up,
      ffn_conv_w, ffn_conv_b, w_down, post_ffn_g)


def kernel(x_prompt, x_sample, pre_mix_g, w_in, conv_w, lam_re, lam_im, log_step, b_re, b_im,
           c_re, c_im, d_skip, w_glu, b_glu, gn_conv, gn_ssm, w_out, post_mix_g,
           pre_ffn_g, w_up, ffn_conv_w, ffn_conv_b, w_down, post_ffn_g):
    assert pre_mix_g.shape[0] == 1, "one encoder layer"
    wv, wy, coef = _s5_tables(lam_re[0], lam_im[0], log_step[0], b_re[0], b_im[0],
                              c_re[0], c_im[0])
    w_in_b = w_in[0].astype(BF16)
    w_out_b = w_out[0].astype(BF16)
    w_up_b = w_up[0].astype(BF16)
    w_down_b = w_down[0].astype(BF16)
    wglu_t = w_glu[0].T.astype(BF16)
    d_col = d_skip[0].reshape(D_SSM, 1)
    bglu_col = b_glu[0].reshape(D_SSM, 1)
    gn_ssm_col = gn_ssm[0].reshape(D_SSM, 1)

    def trunk(x):
        assert x.shape[1] % TILE == 0 and x.shape[2] == D_MODEL
        nconv, zut = _mixer_in(x, pre_mix_g, w_in_b, conv_w[0], gn_conv)
        nssm = _s5_mixer(zut, wv, wy, coef, d_col, wglu_t, bglu_col, gn_ssm_col)
        return _out_ffn(x, nconv, nssm, w_out_b, post_mix_g, pre_ffn_g, w_up_b,
                        ffn_conv_w[0], ffn_conv_b, w_down_b, post_ffn_g)

    return (trunk(x_prompt), trunk(x_sample))
```

```python
import functools

import jax
import jax.numpy as jnp
from jax import lax
from jax.experimental import pallas as pl
from jax.experimental.pallas import tpu as pltpu

D_MODEL = 1024
D_CONV = 512
D_SSM = 512
SSM_GROUP = 16
SSM_GROUPS = D_SSM // SSM_GROUP
SSM_STATE = 64
D_FF = 2816
EPS = 1e-6
LANES = 128

CHUNK = 16
TILE_CHUNKS = 128
TILE = CHUNK * TILE_CHUNKS
HALO = 16
FFN_CHUNKS = 32
FFN_TILE = CHUNK * FFN_CHUNKS
MXU_COLS = 512
FF_BLOCKS = tuple((c0, min(MXU_COLS, D_FF - c0)) for c0 in range(0, D_FF, MXU_COLS))
FF_BLOCK_MAX = MXU_COLS
FF_SPLIT = FF_BLOCKS[-1][0]
PIPE_LAG = 2
SCAN_PAD = 8
PAIR_UNROLL = 4
SCAN_UNROLL = 4
GLU_COLS = 512
VMEM_LIMIT_V7X = 56 * 1024 * 1024

F32 = jnp.float32
BF16 = jnp.bfloat16


def _rms(x, g):
    return x * lax.rsqrt(jnp.mean(x * x, axis=-1, keepdims=True) + EPS) * g


def _dot(a, b):
    return jnp.dot(a, b, preferred_element_type=F32)


def _interleave(heavy, light):
    i = j = 0
    while i < len(heavy) or j < len(light):
        if j >= len(light) or (i < len(heavy) and i * len(light) <= j * len(heavy)):
            heavy[i]()
            i += 1
        else:
            light[j]()
            j += 1


def _const_spec(shape):
    zeros = (0,) * len(shape)
    return pl.BlockSpec(shape, lambda *_: zeros, pipeline_mode=pl.Buffered(1))


def _s5_tables(lam_re, lam_im, log_step, b_re, b_im, c_re, c_im):
    T, G, P, H = CHUNK, SSM_GROUPS, SSM_STATE, SSM_GROUP
    lam = lax.complex(lam_re.astype(F32), lam_im.astype(F32))
    dt = jnp.exp(log_step.astype(F32))[..., None]
    lam_bar = jnp.exp(lam * dt)
    b_bar = ((lam_bar - 1.0) / lam)[..., None] * lax.complex(b_re.astype(F32), b_im.astype(F32))
    cmat = lax.complex(c_re.astype(F32), c_im.astype(F32))
    k = jnp.arange(T + 1, dtype=F32)
    pw = jnp.exp((lam * dt)[..., None] * k)

    kern = jnp.real(jnp.einsum('dgap,dgpk,dgph->dgkah', cmat, pw[..., :T], b_bar))
    tt = jnp.arange(T)
    lag = tt[:, None] - tt[None, :]
    kf = jnp.where((lag >= 0)[None, :, :, None, None], kern[0][:, jnp.clip(lag, 0, T - 1)], 0.0)
    kb = jnp.where((lag <= 0)[None, :, :, None, None], kern[1][:, jnp.clip(-lag, 0, T - 1)], 0.0)
    toe = (kf + kb).transpose(0, 1, 3, 2, 4).reshape(G, T * H, T * H)

    inc_f = pw[0][:, :, T - 1 - tt][..., None] * b_bar[0][:, :, None, :]
    inc_b = pw[1][:, :, tt][..., None] * b_bar[1][:, :, None, :]
    wv = jnp.concatenate([jnp.real(inc_f), jnp.imag(inc_f), jnp.real(inc_b), jnp.imag(inc_b)],
                         axis=1).reshape(G, 4 * P, T * H)

    st_f = cmat[0][:, None] * pw[0][:, :, 1 + tt].transpose(0, 2, 1)[:, :, None, :]
    st_b = cmat[1][:, None] * pw[1][:, :, T - tt].transpose(0, 2, 1)[:, :, None, :]
    mst = jnp.concatenate([jnp.real(st_f), -jnp.imag(st_f), jnp.real(st_b), -jnp.imag(st_b)],
                          axis=-1).reshape(G, T * H, 4 * P)
    wy = jnp.concatenate([toe, mst], axis=-1)

    a16 = pw[..., T].reshape(2, G // 2, 2 * P)
    coef = jnp.stack([jnp.real(a16[0]), jnp.imag(a16[0]), jnp.real(a16[1]), jnp.imag(a16[1])])
    return wv.astype(BF16), wy.astype(BF16), coef.astype(F32)


def _permuted_slab(xcols, t, n_chunks):
    return jnp.concatenate([xc[0, pl.ds(t, n_chunks, stride=CHUNK), :] for xc in xcols], axis=1)


def _column_block_specs(rows, index_map):
    return [pl.BlockSpec((1, rows, LANES), functools.partial(index_map, cb=cb))
            for cb in range(D_MODEL // LANES)]


def _mixer_in_kernel(*refs):
    ncb = D_MODEL // LANES
    xcols = refs[:ncb]
    xp_ref, xn_ref, g_ref, win_ref, cw_ref, gn_ref, nconv_ref, zut_ref, hn_scr = refs[ncb:]
    i = pl.program_id(1)
    last = pl.num_programs(1) - 1
    T, NC = CHUNK, TILE_CHUNKS
    g = g_ref[...]
    for t in range(T):
        xs = _permuted_slab(xcols, t, NC)
        hn_scr[t * NC:(t + 1) * NC, :] = _rms(xs, g).astype(BF16)
    xh = jnp.concatenate([xp_ref[0], xn_ref[0]], axis=0)
    row = lax.broadcasted_iota(jnp.int32, (HALO, 1), 0)
    inside = ((row < 8) & (i > 0)) | ((row >= 8) & (i < last))
    hn_scr[TILE:TILE + HALO, :] = jnp.where(inside, _rms(xh, g), 0.0).astype(BF16)

    z12 = _dot(hn_scr[...], win_ref[:, D_CONV:3 * D_CONV])
    p = z12[:, :D_CONV] * z12[:, D_CONV:]
    zb = _dot(hn_scr[0:TILE, :], win_ref[:, 0:D_CONV])
    w0, w1, w2 = cw_ref[0:1, :], cw_ref[1:2, :], cw_ref[2:3, :]
    gn = gn_ref[...]
    rid = lax.broadcasted_iota(jnp.int32, (NC, 1), 0)
    for t in range(T):
        cur = p[t * NC:(t + 1) * NC]
        if t > 0:
            prev = p[(t - 1) * NC:t * NC]
        else:
            prev = jnp.where(rid == 0, p[TILE + 7:TILE + 8],
                             pltpu.roll(p[(T - 1) * NC:T * NC], 1, 0))
        if t < T - 1:
            nxt = p[(t + 1) * NC:(t + 2) * NC]
        else:
            nxt = jnp.where(rid == NC - 1, p[TILE + 8:TILE + 9],
                            pltpu.roll(p[0:NC], NC - 1, 0))
        yc = zb[t * NC:(t + 1) * NC] * (w0 * prev + w1 * cur + w2 * nxt)
        nconv_ref[0, 0, t * NC:(t + 1) * NC, :] = _rms(yc, gn).astype(BF16)

    zu = _dot(hn_scr[0:TILE, :], win_ref[:, 3 * D_CONV:])
    zut_ref[0, 0] = zu.T.astype(BF16)


def _mixer_in(x, pre_mix_g, w_in, conv_w, gn_conv):
    B, L, D = x.shape
    nt = L // TILE
    blk8 = TILE // 8
    return pl.pallas_call(
        _mixer_in_kernel,
        grid=(B, nt),
        in_specs=_column_block_specs(TILE, lambda b, i, cb: (b, i, cb)) + [
            pl.BlockSpec((1, 8, D), lambda b, i: (b, jnp.maximum(i * blk8 - 1, 0), 0)),
            pl.BlockSpec((1, 8, D), lambda b, i: (b, jnp.minimum((i + 1) * blk8, L // 8 - 1), 0)),
            _const_spec((1, D)),
            _const_spec((D, 4 * D_CONV)),
            _const_spec((3, D_CONV)),
            _const_spec((1, D_CONV)),
        ],
        out_specs=[
            pl.BlockSpec((1, 1, TILE, D_CONV), lambda b, i: (b, i, 0, 0)),
            pl.BlockSpec((1, 1, D_SSM, TILE), lambda b, i: (b, i, 0, 0)),
        ],
        out_shape=[
            jax.ShapeDtypeStruct((B, nt, TILE, D_CONV), BF16),
            jax.ShapeDtypeStruct((B, nt, D_SSM, TILE), BF16),
        ],
        scratch_shapes=[pltpu.VMEM((TILE + HALO, D), BF16)],
        compiler_params=pltpu.CompilerParams(
            dimension_semantics=("parallel", "parallel"),
            vmem_limit_bytes=VMEM_LIMIT_V7X),
        name="mixer_in",
    )(*([x] * (D // LANES)), x, x, pre_mix_g, w_in, conv_w, gn_conv)


def _s5_kernel(zut_ref, wv_ref, wy_ref, coef_ref, d_ref, wglu_ref, bglu_ref, gn_ref,
               out_ref, vfr, vfi, vbr, vbi, yt_scr, *, nt):
    T, NC, H, P = CHUNK, TILE_CHUNKS, SSM_GROUP, SSM_STATE
    npair = SSM_GROUPS // 2
    nc = nt * NC
    pitch = nc + SCAN_PAD

    def chunk_inputs(i, grp):
        r0 = pl.multiple_of(grp * H, H)
        return jnp.concatenate(
            [zut_ref[0, i, pl.ds(r0, H), t * NC:(t + 1) * NC] for t in range(T)], axis=0)

    for i in range(nt):
        def inc_body(q, carry, i=i):
            va = _dot(wv_ref[2 * q], chunk_inputs(i, 2 * q))
            vb = _dot(wv_ref[2 * q + 1], chunk_inputs(i, 2 * q + 1))
            pieces = []
            for kind in range(4):
                pieces += [va[kind * P:(kind + 1) * P], vb[kind * P:(kind + 1) * P]]
            v = jnp.concatenate(pieces, axis=0).T
            r0 = pl.multiple_of(q * pitch + i * NC, 8)
            vfr[pl.ds(r0, NC), :] = v[:, 0:2 * P]
            vfi[pl.ds(r0, NC), :] = v[:, 2 * P:4 * P]
            vbr[pl.ds(r0, NC), :] = v[:, 4 * P:6 * P]
            vbi[pl.ds(r0, NC), :] = v[:, 6 * P:8 * P]
            return carry
        lax.fori_loop(0, npair, inc_body, 0, unroll=PAIR_UNROLL)

    afr, afi, abr, abi = coef_ref[0], coef_ref[1], coef_ref[2], coef_ref[3]

    def scan_body(k, carry):
        sfr, sfi, sbr, sbi = carry
        rows_f = pl.ds(k, npair, stride=pitch)
        rows_b = pl.ds(nc - 1 - k, npair, stride=pitch)
        ur, ui = vfr[rows_f, :], vfi[rows_f, :]
        wr, wi = vbr[rows_b, :], vbi[rows_b, :]
        vfr[rows_f, :] = sfr
        vfi[rows_f, :] = sfi
        vbr[rows_b, :] = sbr
        vbi[rows_b, :] = sbi
        return (afr * sfr - afi * sfi + ur, afr * sfi + afi * sfr + ui,
                abr * sbr - abi * sbi + wr, abr * sbi + abi * sbr + wi)

    zero = jnp.zeros((npair, 2 * P), F32)
    lax.fori_loop(0, nc, scan_body, (zero, zero, zero, zero), unroll=SCAN_UNROLL)

    for i in range(nt):
        def out_body(q, carry, i=i):
            r0 = pl.multiple_of(q * pitch + i * NC, 8)
            rows = pl.ds(r0, NC)
            sin = jnp.concatenate([vfr[rows, :], vfi[rows, :], vbr[rows, :], vbi[rows, :]],
                                  axis=1).T
            for j in range(2):
                grp = 2 * q + j
                st = jnp.concatenate([sin[(2 * kind + j) * P:(2 * kind + j + 1) * P]
                                      for kind in range(4)], axis=0)
                rhs = jnp.concatenate([chunk_inputs(i, grp), st.astype(BF16)], axis=0)
                yt = _dot(wy_ref[grp], rhs)
                h0 = pl.multiple_of(grp * H, H)
                for t in range(T):
                    yt_scr[pl.ds(h0, H), t * NC:(t + 1) * NC] = yt[t * H:(t + 1) * H]
            return carry
        lax.fori_loop(0, npair, out_body, 0, unroll=PAIR_UNROLL)

        for cb in range(TILE // GLU_COLS):
            cols = slice(cb * GLU_COLS, (cb + 1) * GLU_COLS)
            y = yt_scr[:, cols] + d_ref[...] * zut_ref[0, i, :, cols].astype(F32)
            y = jax.nn.gelu(y)
            gate = _dot(wglu_ref[...], y.astype(BF16)) + bglu_ref[...]
            y = y * jax.nn.sigmoid(gate)
            ms = jnp.mean(y * y, axis=0, keepdims=True)
            y = y * lax.rsqrt(ms + EPS) * gn_ref[...]
            out_ref[0, i, cols, :] = y.T.astype(BF16)


def _s5_mixer(zut, wv, wy, coef, d_col, wglu_t, bglu_col, gn_col):
    B, nt = zut.shape[0], zut.shape[1]
    npair = SSM_GROUPS // 2
    scan_rows = npair * (nt * TILE_CHUNKS + SCAN_PAD)
    return pl.pallas_call(
        functools.partial(_s5_kernel, nt=nt),
        grid=(B,),
        in_specs=[
            pl.BlockSpec((1, nt, D_SSM, TILE), lambda b: (b, 0, 0, 0)),
            _const_spec(wv.shape),
            _const_spec(wy.shape),
            _const_spec(coef.shape),
            _const_spec((D_SSM, 1)),
            _const_spec((D_SSM, D_SSM)),
            _const_spec((D_SSM, 1)),
            _const_spec((D_SSM, 1)),
        ],
        out_specs=pl.BlockSpec((1, nt, TILE, D_SSM), lambda b: (b, 0, 0, 0)),
        out_shape=jax.ShapeDtypeStruct((B, nt, TILE, D_SSM), BF16),
        scratch_shapes=[pltpu.VMEM((scan_rows, 2 * SSM_STATE), F32) for _ in range(4)]
        + [pltpu.VMEM((D_SSM, TILE), F32)],
        compiler_params=pltpu.CompilerParams(
            dimension_semantics=("parallel",),
            vmem_limit_bytes=VMEM_LIMIT_V7X),
        name="s5_mixer",
    )(zut, wv, wy, coef, d_col, wglu_t, bglu_col, gn_col)


def _out_ffn_kernel(*refs, nj):
    ncb = D_MODEL // LANES
    xcols = refs[:ncb]
    (xp_ref, xn_ref, nc_ref, ncp_ref, ncn_ref, ns_ref, nsp_ref, nsn_ref,
     wout_ref, pmg_ref, pfg_ref, wup_ref, fcw_ref, fcb_ref, wdown_ref, pog_ref,
     o_ref, up_a, up_b, act_scr, o_scr, x1_new, x1_old, h2_new, h2_old, f_scr,
     mix_scr) = refs[ncb:]
    s = pl.program_id(0)
    ntiles = pl.num_programs(0) - PIPE_LAG
    j = jnp.minimum(s, ntiles - 1) % nj
    last = nj - 1
    T, NC, R = CHUNK, FFN_CHUNKS, FFN_TILE

    @pl.when(s == 0)
    def _():
        for ref in (x1_new, x1_old, h2_new, h2_old, f_scr):
            ref[...] = jnp.zeros(ref.shape, ref.dtype)


    f_prev = f_scr[...]
    f_scale = lax.rsqrt(jnp.mean(f_prev * f_prev, axis=-1, keepdims=True) + EPS)

    def out_piece(cb):
        cols = slice(cb * LANES, (cb + 1) * LANES)
        out = x1_old[:, cols] + f_scr[:, cols] * f_scale * pog_ref[:, cols]
        for t in range(T):
            o_scr[cb, pl.ds(t, NC, stride=T), :] = out[t * NC:(t + 1) * NC]
        o_ref[0, :, cols] = o_scr[cb]
        x1_old[:, cols] = x1_new[:, cols]

    def h2_handover(r):
        rows = slice(r * 4 * NC, (r + 1) * 4 * NC) if r < 4 else slice(R, R + HALO)
        h2_old[rows, :] = h2_new[rows, :]

    out_stage = ([functools.partial(h2_handover, r) for r in range(5)]
                 + [functools.partial(out_piece, cb) for cb in range(ncb)])

    def with_halo(main_ref, prev_ref, next_ref):
        main = main_ref[0, 0].reshape(R, main_ref.shape[-1])
        halo = jnp.concatenate([prev_ref[0, 0, 0].astype(F32)[8:16],
                                next_ref[0, 0, 0].astype(F32)[0:8]], axis=0).astype(BF16)
        return jnp.concatenate([main, halo], axis=0)

    def proj_piece(c):
        cols = slice(c * MXU_COLS, (c + 1) * MXU_COLS)
        lhs = jnp.concatenate([with_halo(nc_ref, ncp_ref, ncn_ref),
                               with_halo(ns_ref, nsp_ref, nsn_ref)], axis=1)
        mix_scr[:, cols] = _dot(lhs, wout_ref[:, cols])

    def norm_piece(r):
        if r < T:
            rows = slice(r * NC, (r + 1) * NC)
            xr = _permuted_slab(xcols, r, NC)
            x1 = xr + _rms(mix_scr[rows, :], pmg_ref[...])
            x1_new[rows, :] = x1
            h2_new[rows, :] = _rms(x1, pfg_ref[...]).astype(BF16)
        else:
            rows = slice(R, R + HALO)
            xr = jnp.concatenate([xp_ref[0], xn_ref[0]], axis=0)
            x1 = xr + _rms(mix_scr[rows, :], pmg_ref[...])
            row = lax.broadcasted_iota(jnp.int32, (HALO, 1), 0)
            inside = ((row < 8) & (j > 0)) | ((row >= 8) & (j < last))
            h2_new[rows, :] = jnp.where(inside, _rms(x1, pfg_ref[...]), 0.0).astype(BF16)

    proj_stage = [functools.partial(proj_piece, c) for c in range(D_MODEL // MXU_COLS)]
    norm_stage = [functools.partial(norm_piece, r) for r in range(T + 1)]

    rid = lax.broadcasted_iota(jnp.int32, (NC, 1), 0)

    def up_piece(k, part):
        c0, width = FF_BLOCKS[k]
        up = (up_a, up_b)[k % 2]
        src = slice(part * D_FF + c0, part * D_FF + c0 + width)
        up[:, part * width:(part + 1) * width] = _dot(h2_old[...], wup_ref[:, src])

    def conv_piece(k, t):
        c0, width = FF_BLOCKS[k]
        up = (up_a, up_b)[k % 2]
        gcols = slice(c0, c0 + width)
        vcols = slice(D_FF + c0, D_FF + c0 + width)
        w0 = jnp.concatenate([fcw_ref[0:1, gcols], fcw_ref[0:1, vcols]], axis=1)
        w1 = jnp.concatenate([fcw_ref[1:2, gcols], fcw_ref[1:2, vcols]], axis=1)
        w2 = jnp.concatenate([fcw_ref[2:3, gcols], fcw_ref[2:3, vcols]], axis=1)
        bias = jnp.concatenate([fcb_ref[:, gcols], fcb_ref[:, vcols]], axis=1)
        cols = slice(0, 2 * width)
        cur = up[t * NC:(t + 1) * NC, cols]
        if t > 0:
            prev = up[(t - 1) * NC:t * NC, cols]
        else:
            prev = jnp.where(rid == 0, up[R + 7:R + 8, cols],
                             pltpu.roll(up[(T - 1) * NC:T * NC, cols], 1, 0))
        if t < T - 1:
            nxt = up[(t + 1) * NC:(t + 2) * NC, cols]
        else:
            nxt = jnp.where(rid == NC - 1, up[R + 8:R + 9, cols],
                            pltpu.roll(up[0:NC, cols], NC - 1, 0))
        u = w0 * prev + w1 * cur + w2 * nxt + bias
        act = jax.nn.silu(u[:, :width]) * u[:, width:]
        act_scr[t * NC:(t + 1) * NC, gcols] = act.astype(BF16)

    def down_piece(first, c):
        cols = slice(c * MXU_COLS, (c + 1) * MXU_COLS)
        if first:
            f_scr[:, cols] = _dot(act_scr[:, 0:FF_SPLIT], wdown_ref[0:FF_SPLIT, cols])
        else:
            f_scr[:, cols] += _dot(act_scr[:, FF_SPLIT:], wdown_ref[FF_SPLIT:, cols])

    nblk = len(FF_BLOCKS)
    up_stage = [[functools.partial(up_piece, k, part) for part in range(2)] for k in range(nblk)]
    conv_stage = [[functools.partial(conv_piece, k, t) for t in range(T)] for k in range(nblk)]
    down_stage = [[functools.partial(down_piece, first, c) for c in range(D_MODEL // MXU_COLS)]
                  for first in (True, False)]
    assert FF_SPLIT <= FF_BLOCKS[-1][0]

    _interleave(proj_stage, out_stage)
    _interleave(up_stage[0], norm_stage)
    for k in range(nblk):
        heavy = up_stage[k + 1] if k + 1 < nblk else down_stage[0]
        _interleave(heavy, conv_stage[k])
    _interleave(down_stage[1], [])


def _out_ffn(x, nconv, nssm, w_out, post_mix_g, pre_ffn_g, w_up, ffn_conv_w, ffn_conv_b,
             w_down, post_ffn_g):
    B, L, D = x.shape
    nt = L // TILE
    nj = L // FFN_TILE
    per_tile = TILE_CHUNKS // FFN_CHUNKS
    blk8 = FFN_TILE // 8
    nchunks = L // CHUNK
    nc5 = nconv.reshape(B, nt, CHUNK, TILE_CHUNKS, D_CONV)
    ns5 = nssm.reshape(B, nt, CHUNK, TILE_CHUNKS, D_SSM)

    ntiles = B * nj

    def in_tile(s):
        tile = jnp.minimum(s, ntiles - 1)
        return tile // nj, tile % nj

    def out_map(s):
        tile = jnp.maximum(s - PIPE_LAG, 0)
        return (tile // nj, tile % nj, 0)

    def x_map(s, cb):
        b, j = in_tile(s)
        return (b, j, cb)

    def x_prev_map(s):
        b, j = in_tile(s)
        return (b, jnp.maximum(j * blk8 - 1, 0), 0)

    def x_next_map(s):
        b, j = in_tile(s)
        return (b, jnp.minimum((j + 1) * blk8, L // 8 - 1), 0)

    def main_map(s):
        b, j = in_tile(s)
        return (b, j // per_tile, 0, j % per_tile, 0)

    def prev_map(s):
        b, j = in_tile(s)
        c = jnp.maximum(j * FFN_CHUNKS - 1, 0)
        return (b, c // TILE_CHUNKS, CHUNK - 1, (c % TILE_CHUNKS) // 16, 0)

    def next_map(s):
        b, j = in_tile(s)
        c = jnp.minimum((j + 1) * FFN_CHUNKS, nchunks - 1)
        return (b, c // TILE_CHUNKS, 0, (c % TILE_CHUNKS) // 16, 0)

    act_specs = []
    for width in (D_CONV, D_SSM):
        act_specs += [pl.BlockSpec((1, 1, CHUNK, FFN_CHUNKS, width), main_map),
                      pl.BlockSpec((1, 1, 1, 16, width), prev_map),
                      pl.BlockSpec((1, 1, 1, 16, width), next_map)]
    return pl.pallas_call(
        functools.partial(_out_ffn_kernel, nj=nj),
        grid=(ntiles + PIPE_LAG,),
        in_specs=_column_block_specs(FFN_TILE, x_map) + [
            pl.BlockSpec((1, 8, D), x_prev_map),
            pl.BlockSpec((1, 8, D), x_next_map),
        ] + act_specs + [
            _const_spec((D_CONV + D_SSM, D)),
            _const_spec((1, D)),
            _const_spec((1, D)),
            _const_spec((D, 2 * D_FF)),
            _const_spec((3, 2 * D_FF)),
            _const_spec((1, 2 * D_FF)),
            _const_spec((D_FF, D)),
            _const_spec((1, D)),
        ],
        out_specs=pl.BlockSpec((1, FFN_TILE, D), out_map),
        out_shape=jax.ShapeDtypeStruct((B, L, D), F32),
        scratch_shapes=[pltpu.VMEM((FFN_TILE + HALO, 2 * FF_BLOCK_MAX), F32),
                        pltpu.VMEM((FFN_TILE + HALO, 2 * FF_BLOCK_MAX), F32),
                        pltpu.VMEM((FFN_TILE, D_FF), BF16),
                        pltpu.VMEM((D // LANES, FFN_TILE, LANES), F32),
                        pltpu.VMEM((FFN_TILE, D), F32),
                        pltpu.VMEM((FFN_TILE, D), F32),
                        pltpu.VMEM((FFN_TILE + HALO, D), BF16),
                        pltpu.VMEM((FFN_TILE + HALO, D), BF16),
                        pltpu.VMEM((FFN_TILE, D), F32),
                        pltpu.VMEM((FFN_TILE + HALO, D), F32)],
        compiler_params=pltpu.CompilerParams(
            dimension_semantics=("arbitrary",),
            vmem_limit_bytes=VMEM_LIMIT_V7X),
        name="out_ffn",
    )(*([x] * (D // LANES)), x, x, nc5, nc5, nc5, ns5, ns5, ns5, w_out, post_mix_g, pre_ffn_g, w_up,
      ffn_conv_w, ffn_conv_b, w_down, post_ffn_g)


def kernel(x_prompt, x_sample, pre_mix_g, w_in, conv_w, lam_re, lam_im, log_step, b_re, b_im,
           c_re, c_im, d_skip, w_glu, b_glu, gn_conv, gn_ssm, w_out, post_mix_g,
           pre_ffn_g, w_up, ffn_conv_w, ffn_conv_b, w_down, post_ffn_g):
    assert pre_mix_g.shape[0] == 1, "one encoder layer"
    wv, wy, coef = _s5_tables(lam_re[0], lam_im[0], log_step[0], b_re[0], b_im[0],
                              c_re[0], c_im[0])
    w_in_b = w_in[0].astype(BF16)
    w_out_b = w_out[0].astype(BF16)
    w_up_b = w_up[0].astype(BF16)
    w_down_b = w_down[0].astype(BF16)
    wglu_t = w_glu[0].T.astype(BF16)
    d_col = d_skip[0].reshape(D_SSM, 1)
    bglu_col = b_glu[0].reshape(D_SSM, 1)
    gn_ssm_col = gn_ssm[0].reshape(D_SSM, 1)

    def trunk(x):
        assert x.shape[1] % TILE == 0 and x.shape[2] == D_MODEL
        nconv, zut = _mixer_in(x, pre_mix_g, w_in_b, conv_w[0], gn_conv)
        nssm = _s5_mixer(zut, wv, wy, coef, d_col, wglu_t, bglu_col, gn_ssm_col)
        return _out_ffn(x, nconv, nssm, w_out_b, post_mix_g, pre_ffn_g, w_up_b,
                        ffn_conv_w[0], ffn_conv_b, w_down_b, post_ffn_g)

    return (trunk(x_prompt), trunk(x_sample))
```

```python
import functools

import jax
import jax.numpy as jnp
from jax import lax
from jax.experimental import pallas as pl
from jax.experimental.pallas import tpu as pltpu

D_MODEL = 1024
D_CONV = 512
D_SSM = 512
SSM_GROUP = 16
SSM_GROUPS = D_SSM // SSM_GROUP
SSM_STATE = 64
D_FF = 2816
EPS = 1e-6
LANES = 128

CHUNK = 16
TILE_CHUNKS = 128
TILE = CHUNK * TILE_CHUNKS
HALO = 16
FFN_CHUNKS = 32
FFN_TILE = CHUNK * FFN_CHUNKS
MXU_COLS = 512
FF_BLOCKS = tuple((c0, min(MXU_COLS, D_FF - c0)) for c0 in range(0, D_FF, MXU_COLS))
FF_BLOCK_MAX = MXU_COLS
FF_SPLIT = FF_BLOCKS[-1][0]
PIPE_LAG = 2
SCAN_PAD = 8
PAIR_UNROLL = 4
SCAN_UNROLL = 4
GLU_COLS = 512
VMEM_LIMIT_V7X = 56 * 1024 * 1024

F32 = jnp.float32
BF16 = jnp.bfloat16


def _rms(x, g):
    return x * lax.rsqrt(jnp.mean(x * x, axis=-1, keepdims=True) + EPS) * g


def _dot(a, b):
    return jnp.dot(a, b, preferred_element_type=F32)


def _interleave(heavy, light):
    i = j = 0
    while i < len(heavy) or j < len(light):
        if j >= len(light) or (i < len(heavy) and i * len(light) <= j * len(heavy)):
            heavy[i]()
            i += 1
        else:
            light[j]()
            j += 1


def _const_spec(shape):
    zeros = (0,) * len(shape)
    return pl.BlockSpec(shape, lambda *_: zeros, pipeline_mode=pl.Buffered(1))


def _s5_tables(lam_re, lam_im, log_step, b_re, b_im, c_re, c_im):
    T, G, P, H = CHUNK, SSM_GROUPS, SSM_STATE, SSM_GROUP
    lam = lax.complex(lam_re.astype(F32), lam_im.astype(F32))
    dt = jnp.exp(log_step.astype(F32))[..., None]
    lam_bar = jnp.exp(lam * dt)
    b_bar = ((lam_bar - 1.0) / lam)[..., None] * lax.complex(b_re.astype(F32), b_im.astype(F32))
    cmat = lax.complex(c_re.astype(F32), c_im.astype(F32))
    k = jnp.arange(T + 1, dtype=F32)
    pw = jnp.exp((lam * dt)[..., None] * k)

    kern = jnp.real(jnp.einsum('dgap,dgpk,dgph->dgkah', cmat, pw[..., :T], b_bar))
    tt = jnp.arange(T)
    lag = tt[:, None] - tt[None, :]
    kf = jnp.where((lag >= 0)[None, :, :, None, None], kern[0][:, jnp.clip(lag, 0, T - 1)], 0.0)
    kb = jnp.where((lag <= 0)[None, :, :, None, None], kern[1][:, jnp.clip(-lag, 0, T - 1)], 0.0)
    toe = (kf + kb).transpose(0, 1, 3, 2, 4).reshape(G, T * H, T * H)

    inc_f = pw[0][:, :, T - 1 - tt][..., None] * b_bar[0][:, :, None, :]
    inc_b = pw[1][:, :, tt][..., None] * b_bar[1][:, :, None, :]
    wv = jnp.concatenate([jnp.real(inc_f), jnp.imag(inc_f), jnp.real(inc_b), jnp.imag(inc_b)],
                         axis=1).reshape(G, 4 * P, T * H)

    st_f = cmat[0][:, None] * pw[0][:, :, 1 + tt].transpose(0, 2, 1)[:, :, None, :]
    st_b = cmat[1][:, None] * pw[1][:, :, T - tt].transpose(0, 2, 1)[:, :, None, :]
    mst = jnp.concatenate([jnp.real(st_f), -jnp.imag(st_f), jnp.real(st_b), -jnp.imag(st_b)],
                          axis=-1).reshape(G, T * H, 4 * P)
    wy = jnp.concatenate([toe, mst], axis=-1)

    a16 = pw[..., T].reshape(2, G // 2, 2 * P)
    coef = jnp.stack([jnp.real(a16[0]), jnp.imag(a16[0]), jnp.real(a16[1]), jnp.imag(a16[1])])
    return wv.astype(BF16), wy.astype(BF16), coef.astype(F32)


def _permuted_slab(xcols, t, n_chunks):
    return jnp.concatenate([xc[0, pl.ds(t, n_chunks, stride=CHUNK), :] for xc in xcols], axis=1)


def _column_block_specs(rows, index_map):
    return [pl.BlockSpec((1, rows, LANES), functools.partial(index_map, cb=cb))
            for cb in range(D_MODEL // LANES)]


def _mixer_in_kernel(*refs):
    ncb = D_MODEL // LANES
    xcols = refs[:ncb]
    xp_ref, xn_ref, g_ref, win_ref, cw_ref, gn_ref, nconv_ref, zut_ref, hn_scr = refs[ncb:]
    i = pl.program_id(1)
    last = pl.num_programs(1) - 1
    T, NC = CHUNK, TILE_CHUNKS
    g = g_ref[...]
    for t in range(T):
        xs = _permuted_slab(xcols, t, NC)
        hn_scr[t * NC:(t + 1) * NC, :] = _rms(xs, g).astype(BF16)
    xh = jnp.concatenate([xp_ref[0], xn_ref[0]], axis=0)
    row = lax.broadcasted_iota(jnp.int32, (HALO, 1), 0)
    inside = ((row < 8) & (i > 0)) | ((row >= 8) & (i < last))
    hn_scr[TILE:TILE + HALO, :] = jnp.where(inside, _rms(xh, g), 0.0).astype(BF16)

    z12 = _dot(hn_scr[...], win_ref[:, D_CONV:3 * D_CONV])
    p = z12[:, :D_CONV] * z12[:, D_CONV:]
    zb = _dot(hn_scr[0:TILE, :], win_ref[:, 0:D_CONV])
    w0, w1, w2 = cw_ref[0:1, :], cw_ref[1:2, :], cw_ref[2:3, :]
    gn = gn_ref[...]
    rid = lax.broadcasted_iota(jnp.int32, (NC, 1), 0)
    for t in range(T):
        cur = p[t * NC:(t + 1) * NC]
        if t > 0:
            prev = p[(t - 1) * NC:t * NC]
        else:
            prev = jnp.where(rid == 0, p[TILE + 7:TILE + 8],
                             pltpu.roll(p[(T - 1) * NC:T * NC], 1, 0))
        if t < T - 1:
            nxt = p[(t + 1) * NC:(t + 2) * NC]
        else:
            nxt = jnp.where(rid == NC - 1, p[TILE + 8:TILE + 9],
                            pltpu.roll(p[0:NC], NC - 1, 0))
        yc = zb[t * NC:(t + 1) * NC] * (w0 * prev + w1 * cur + w2 * nxt)
        nconv_ref[0, 0, t * NC:(t + 1) * NC, :] = _rms(yc, gn).astype(BF16)

    zu = _dot(hn_scr[0:TILE, :], win_ref[:, 3 * D_CONV:])
    zut_ref[0, 0] = zu.T.astype(BF16)


def _mixer_in(x, pre_mix_g, w_in, conv_w, gn_conv):
    B, L, D = x.shape
    nt = L // TILE
    blk8 = TILE // 8
    return pl.pallas_call(
        _mixer_in_kernel,
        grid=(B, nt),
        in_specs=_column_block_specs(TILE, lambda b, i, cb: (b, i, cb)) + [
            pl.BlockSpec((1, 8, D), lambda b, i: (b, jnp.maximum(i * blk8 - 1, 0), 0)),
            pl.BlockSpec((1, 8, D), lambda b, i: (b, jnp.minimum((i + 1) * blk8, L // 8 - 1), 0)),
            _const_spec((1, D)),
            _const_spec((D, 4 * D_CONV)),
            _const_spec((3, D_CONV)),
            _const_spec((1, D_CONV)),
        ],
        out_specs=[
            pl.BlockSpec((1, 1, TILE, D_CONV), lambda b, i: (b, i, 0, 0)),
            pl.BlockSpec((1, 1, D_SSM, TILE), lambda b, i: (b, i, 0, 0)),
        ],
        out_shape=[
            jax.ShapeDtypeStruct((B, nt, TILE, D_CONV), BF16),
            jax.ShapeDtypeStruct((B, nt, D_SSM, TILE), BF16),
        ],
        scratch_shapes=[pltpu.VMEM((TILE + HALO, D), BF16)],
        compiler_params=pltpu.CompilerParams(
            dimension_semantics=("parallel", "parallel"),
            vmem_limit_bytes=VMEM_LIMIT_V7X),
        name="mixer_in",
    )(*([x] * (D // LANES)), x, x, pre_mix_g, w_in, conv_w, gn_conv)


def _s5_kernel(zut_ref, wv_ref, wy_ref, coef_ref, d_ref, wglu_ref, bglu_ref, gn_ref,
               out_ref, vfr, vfi, vbr, vbi, yt_scr, *, nt):
    T, NC, H, P = CHUNK, TILE_CHUNKS, SSM_GROUP, SSM_STATE
    npair = SSM_GROUPS // 2
    nc = nt * NC
    pitch = nc + SCAN_PAD

    def chunk_inputs(i, grp):
        r0 = pl.multiple_of(grp * H, H)
        return jnp.concatenate(
            [zut_ref[0, i, pl.ds(r0, H), t * NC:(t + 1) * NC] for t in range(T)], axis=0)

    for i in range(nt):
        def inc_body(q, carry, i=i):
            va = _dot(wv_ref[2 * q], chunk_inputs(i, 2 * q))
            vb = _dot(wv_ref[2 * q + 1], chunk_inputs(i, 2 * q + 1))
            pieces = []
            for kind in range(4):
                pieces += [va[kind * P:(kind + 1) * P], vb[kind * P:(kind + 1) * P]]
            v = jnp.concatenate(pieces, axis=0).T
            r0 = pl.multiple_of(q * pitch + i * NC, 8)
            vfr[pl.ds(r0, NC), :] = v[:, 0:2 * P]
            vfi[pl.ds(r0, NC), :] = v[:, 2 * P:4 * P]
            vbr[pl.ds(r0, NC), :] = v[:, 4 * P:6 * P]
            vbi[pl.ds(r0, NC), :] = v[:, 6 * P:8 * P]
            return carry
        lax.fori_loop(0, npair, inc_body, 0, unroll=PAIR_UNROLL)

    afr, afi, abr, abi = coef_ref[0], coef_ref[1], coef_ref[2], coef_ref[3]

    def scan_body(k, carry):
        sfr, sfi, sbr, sbi = carry
        rows_f = pl.ds(k, npair, stride=pitch)
        rows_b = pl.ds(nc - 1 - k, npair, stride=pitch)
        ur, ui = vfr[rows_f, :], vfi[rows_f, :]
        wr, wi = vbr[rows_b, :], vbi[rows_b, :]
        vfr[rows_f, :] = sfr
        vfi[rows_f, :] = sfi
        vbr[rows_b, :] = sbr
        vbi[rows_b, :] = sbi
        return (afr * sfr - afi * sfi + ur, afr * sfi + afi * sfr + ui,
                abr * sbr - abi * sbi + wr, abr * sbi + abi * sbr + wi)

    zero = jnp.zeros((npair, 2 * P), F32)
    lax.fori_loop(0, nc, scan_body, (zero, zero, zero, zero), unroll=SCAN_UNROLL)

    for i in range(nt):
        def out_body(q, carry, i=i):
            r0 = pl.multiple_of(q * pitch + i * NC, 8)
            rows = pl.ds(r0, NC)
            sin = jnp.concatenate([vfr[rows, :], vfi[rows, :], vbr[rows, :], vbi[rows, :]],
                                  axis=1).T
            for j in range(2):
                grp = 2 * q + j
                st = jnp.concatenate([sin[(2 * kind + j) * P:(2 * kind + j + 1) * P]
                                      for kind in range(4)], axis=0)
                rhs = jnp.concatenate([chunk_inputs(i, grp), st.astype(BF16)], axis=0)
                yt = _dot(wy_ref[grp], rhs)
                h0 = pl.multiple_of(grp * H, H)
                for t in range(T):
                    yt_scr[pl.ds(h0, H), t * NC:(t + 1) * NC] = yt[t * H:(t + 1) * H]
            return carry
        lax.fori_loop(0, npair, out_body, 0, unroll=PAIR_UNROLL)

        for cb in range(TILE // GLU_COLS):
            cols = slice(cb * GLU_COLS, (cb + 1) * GLU_COLS)
            y = yt_scr[:, cols] + d_ref[...] * zut_ref[0, i, :, cols].astype(F32)
            y = jax.nn.gelu(y)
            gate = _dot(wglu_ref[...], y.astype(BF16)) + bglu_ref[...]
            y = y * jax.nn.sigmoid(gate)
            ms = jnp.mean(y * y, axis=0, keepdims=True)
            y = y * lax.rsqrt(ms + EPS) * gn_ref[...]
            out_ref[0, i, cols, :] = y.T.astype(BF16)


def _s5_mixer(zut, wv, wy, coef, d_col, wglu_t, bglu_col, gn_col):
    B, nt = zut.shape[0], zut.shape[1]
    npair = SSM_GROUPS // 2
    scan_rows = npair * (nt * TILE_CHUNKS + SCAN_PAD)
    return pl.pallas_call(
        functools.partial(_s5_kernel, nt=nt),
        grid=(B,),
        in_specs=[
            pl.BlockSpec((1, nt, D_SSM, TILE), lambda b: (b, 0, 0, 0)),
            _const_spec(wv.shape),
            _const_spec(wy.shape),
            _const_spec(coef.shape),
            _const_spec((D_SSM, 1)),
            _const_spec((D_SSM, D_SSM)),
            _const_spec((D_SSM, 1)),
            _const_spec((D_SSM, 1)),
        ],
        out_specs=pl.BlockSpec((1, nt, TILE, D_SSM), lambda b: (b, 0, 0, 0)),
        out_shape=jax.ShapeDtypeStruct((B, nt, TILE, D_SSM), BF16),
        scratch_shapes=[pltpu.VMEM((scan_rows, 2 * SSM_STATE), F32) for _ in range(4)]
        + [pltpu.VMEM((D_SSM, TILE), F32)],
        compiler_params=pltpu.CompilerParams(
            dimension_semantics=("parallel",),
            vmem_limit_bytes=VMEM_LIMIT_V7X),
        name="s5_mixer",
    )(zut, wv, wy, coef, d_col, wglu_t, bglu_col, gn_col)


def _out_ffn_kernel(*refs, nj):
    ncb = D_MODEL // LANES
    xcols = refs[:ncb]
    (xp_ref, xn_ref, nc_ref, ncp_ref, ncn_ref, ns_ref, nsp_ref, nsn_ref,
     wout_ref, pmg_ref, pfg_ref, wup_ref, fcw_ref, fcb_ref, wdown_ref, pog_ref,
     o_ref, up_a, up_b, act_scr, x1_even, x1_odd, h2_even, h2_odd, f_scr,
     mix_scr) = refs[ncb:]
    assert up_b.shape[0] >= ncb
    s = pl.program_id(0)
    ntiles = pl.num_programs(0) - PIPE_LAG

    @pl.when(s == 0)
    def _():
        for ref in (x1_even, x1_odd, h2_even, h2_odd, f_scr):
            ref[...] = jnp.zeros(ref.shape, ref.dtype)

    common = refs[:ncb] + refs[ncb:ncb + 20] + (f_scr, mix_scr)
    for parity, (x1_tile, h2_write, h2_read) in enumerate(
            ((x1_even, h2_even, h2_odd), (x1_odd, h2_odd, h2_even))):
        @pl.when(s % 2 == parity)
        def _(x1_tile=x1_tile, h2_write=h2_write, h2_read=h2_read):
            _out_ffn_step(common, x1_tile, h2_write, h2_read, s, ntiles, nj)


def _out_ffn_step(common, x1_tile, h2_new, h2_old, s, ntiles, nj):
    ncb = D_MODEL // LANES
    xcols = common[:ncb]
    (xp_ref, xn_ref, nc_ref, ncp_ref, ncn_ref, ns_ref, nsp_ref, nsn_ref,
     wout_ref, pmg_ref, pfg_ref, wup_ref, fcw_ref, fcb_ref, wdown_ref, pog_ref,
     o_ref, up_a, up_b, act_scr, f_scr, mix_scr) = common[ncb:]
    j = jnp.minimum(s, ntiles - 1) % nj
    last = nj - 1
    T, NC, R = CHUNK, FFN_CHUNKS, FFN_TILE


    f_prev = f_scr[...]
    f_scale = lax.rsqrt(jnp.mean(f_prev * f_prev, axis=-1, keepdims=True) + EPS)

    def out_piece(cb):
        cols = slice(cb * LANES, (cb + 1) * LANES)
        out = x1_tile[:, cols] + f_scr[:, cols] * f_scale * pog_ref[:, cols]
        for t in range(T):
            up_b[cb, pl.ds(t, NC, stride=T), :] = out[t * NC:(t + 1) * NC]
        o_ref[0, :, cols] = up_b[cb, 0:R, :]

    out_stage = [functools.partial(out_piece, cb) for cb in range(ncb)]

    def with_halo(main_ref, prev_ref, next_ref):
        main = main_ref[0, 0].reshape(R, main_ref.shape[-1])
        halo = jnp.concatenate([prev_ref[0, 0, 0].astype(F32)[8:16],
                                next_ref[0, 0, 0].astype(F32)[0:8]], axis=0).astype(BF16)
        return jnp.concatenate([main, halo], axis=0)

    def proj_piece(c):
        cols = slice(c * MXU_COLS, (c + 1) * MXU_COLS)
        lhs = jnp.concatenate([with_halo(nc_ref, ncp_ref, ncn_ref),
                               with_halo(ns_ref, nsp_ref, nsn_ref)], axis=1)
        mix_scr[:, cols] = _dot(lhs, wout_ref[:, cols])

    def norm_piece(r):
        if r < T:
            rows = slice(r * NC, (r + 1) * NC)
            xr = _permuted_slab(xcols, r, NC)
            x1 = xr + _rms(mix_scr[rows, :], pmg_ref[...])
            x1_tile[rows, :] = x1
            h2_new[rows, :] = _rms(x1, pfg_ref[...]).astype(BF16)
        else:
            rows = slice(R, R + HALO)
            xr = jnp.concatenate([xp_ref[0], xn_ref[0]], axis=0)
            x1 = xr + _rms(mix_scr[rows, :], pmg_ref[...])
            row = lax.broadcasted_iota(jnp.int32, (HALO, 1), 0)
            inside = ((row < 8) & (j > 0)) | ((row >= 8) & (j < last))
            h2_new[rows, :] = jnp.where(inside, _rms(x1, pfg_ref[...]), 0.0).astype(BF16)

    proj_stage = [functools.partial(proj_piece, c) for c in range(D_MODEL // MXU_COLS)]
    norm_stage = [functools.partial(norm_piece, r) for r in range(T + 1)]

    rid = lax.broadcasted_iota(jnp.int32, (NC, 1), 0)

    def up_piece(k, part):
        c0, width = FF_BLOCKS[k]
        up = (up_a, up_b)[k % 2]
        res = _dot(h2_old[...], wup_ref[:, part * D_FF + c0:part * D_FF + c0 + width])
        nq = width // LANES
        for q in range(nq):
            up[part * nq + q] = res[:, q * LANES:(q + 1) * LANES]

    def conv_taps(up, blk, cols, t):
        cur = up[blk, t * NC:(t + 1) * NC, :]
        if t > 0:
            prev = up[blk, (t - 1) * NC:t * NC, :]
        else:
            prev = jnp.where(rid == 0, up[blk, R + 7:R + 8, :],
                             pltpu.roll(up[blk, (T - 1) * NC:T * NC, :], 1, 0))
        if t < T - 1:
            nxt = up[blk, (t + 1) * NC:(t + 2) * NC, :]
        else:
            nxt = jnp.where(rid == NC - 1, up[blk, R + 8:R + 9, :],
                            pltpu.roll(up[blk, 0:NC, :], NC - 1, 0))
        return (fcw_ref[0:1, cols] * prev + fcw_ref[1:2, cols] * cur + fcw_ref[2:3, cols] * nxt
                + fcb_ref[:, cols])

    def conv_piece(k, t):
        c0, width = FF_BLOCKS[k]
        up = (up_a, up_b)[k % 2]
        nq = width // LANES
        for q in range(nq):
            gcols = slice(c0 + q * LANES, c0 + (q + 1) * LANES)
            vcols = slice(D_FF + c0 + q * LANES, D_FF + c0 + (q + 1) * LANES)
            act = jax.nn.silu(conv_taps(up, q, gcols, t)) * conv_taps(up, nq + q, vcols, t)
            act_scr[t * NC:(t + 1) * NC, gcols] = act.astype(BF16)

    def down_piece(first, c):
        cols = slice(c * MXU_COLS, (c + 1) * MXU_COLS)
        if first:
            f_scr[:, cols] = _dot(act_scr[:, 0:FF_SPLIT], wdown_ref[0:FF_SPLIT, cols])
        else:
            f_scr[:, cols] += _dot(act_scr[:, FF_SPLIT:], wdown_ref[FF_SPLIT:, cols])

    nblk = len(FF_BLOCKS)
    up_stage = [[functools.partial(up_piece, k, part) for part in range(2)] for k in range(nblk)]
    conv_stage = [[functools.partial(conv_piece, k, t) for t in range(T)] for k in range(nblk)]
    down_stage = [[functools.partial(down_piece, first, c) for c in range(D_MODEL // MXU_COLS)]
                  for first in (True, False)]
    assert FF_SPLIT <= FF_BLOCKS[-1][0]

    _interleave(proj_stage, out_stage)
    _interleave(up_stage[0], norm_stage)
    for k in range(nblk):
        heavy = up_stage[k + 1] if k + 1 < nblk else down_stage[0]
        _interleave(heavy, conv_stage[k])
    _interleave(down_stage[1], [])


def _out_ffn(x, nconv, nssm, w_out, post_mix_g, pre_ffn_g, w_up, ffn_conv_w, ffn_conv_b,
             w_down, post_ffn_g):
    B, L, D = x.shape
    nt = L // TILE
    nj = L // FFN_TILE
    per_tile = TILE_CHUNKS // FFN_CHUNKS
    blk8 = FFN_TILE // 8
    nchunks = L // CHUNK
    nc5 = nconv.reshape(B, nt, CHUNK, TILE_CHUNKS, D_CONV)
    ns5 = nssm.reshape(B, nt, CHUNK, TILE_CHUNKS, D_SSM)

    ntiles = B * nj
    up_shape = (2 * FF_BLOCK_MAX // LANES, FFN_TILE + HALO, LANES)

    def in_tile(s):
        tile = jnp.minimum(s, ntiles - 1)
        return tile // nj, tile % nj

    def out_map(s):
        tile = jnp.maximum(s - PIPE_LAG, 0)
        return (tile // nj, tile % nj, 0)

    def x_map(s, cb):
        b, j = in_tile(s)
        return (b, j, cb)

    def x_prev_map(s):
        b, j = in_tile(s)
        return (b, jnp.maximum(j * blk8 - 1, 0), 0)

    def x_next_map(s):
        b, j = in_tile(s)
        return (b, jnp.minimum((j + 1) * blk8, L // 8 - 1), 0)

    def main_map(s):
        b, j = in_tile(s)
        return (b, j // per_tile, 0, j % per_tile, 0)

    def prev_map(s):
        b, j = in_tile(s)
        c = jnp.maximum(j * FFN_CHUNKS - 1, 0)
        return (b, c // TILE_CHUNKS, CHUNK - 1, (c % TILE_CHUNKS) // 16, 0)

    def next_map(s):
        b, j = in_tile(s)
        c = jnp.minimum((j + 1) * FFN_CHUNKS, nchunks - 1)
        return (b, c // TILE_CHUNKS, 0, (c % TILE_CHUNKS) // 16, 0)

    act_specs = []
    for width in (D_CONV, D_SSM):
        act_specs += [pl.BlockSpec((1, 1, CHUNK, FFN_CHUNKS, width), main_map),
                      pl.BlockSpec((1, 1, 1, 16, width), prev_map),
                      pl.BlockSpec((1, 1, 1, 16, width), next_map)]
    return pl.pallas_call(
        functools.partial(_out_ffn_kernel, nj=nj),
        grid=(ntiles + PIPE_LAG,),
        in_specs=_column_block_specs(FFN_TILE, x_map) + [
            pl.BlockSpec((1, 8, D), x_prev_map),
            pl.BlockSpec((1, 8, D), x_next_map),
        ] + act_specs + [
            _const_spec((D_CONV + D_SSM, D)),
            _const_spec((1, D)),
            _const_spec((1, D)),
            _const_spec((D, 2 * D_FF)),
            _const_spec((3, 2 * D_FF)),
            _const_spec((1, 2 * D_FF)),
            _const_spec((D_FF, D)),
            _const_spec((1, D)),
        ],
        out_specs=pl.BlockSpec((1, FFN_TILE, D), out_map),
        out_shape=jax.ShapeDtypeStruct((B, L, D), F32),
        scratch_shapes=[pltpu.VMEM(up_shape, F32),
                        pltpu.VMEM(up_shape, F32),
                        pltpu.VMEM((FFN_TILE, D_FF), BF16),
                        pltpu.VMEM((FFN_TILE, D), F32),
                        pltpu.VMEM((FFN_TILE, D), F32),
                        pltpu.VMEM((FFN_TILE + HALO, D), BF16),
                        pltpu.VMEM((FFN_TILE + HALO, D), BF16),
                        pltpu.VMEM((FFN_TILE, D), F32),
                        pltpu.VMEM((FFN_TILE + HALO, D), F32)],
        compiler_params=pltpu.CompilerParams(
            dimension_semantics=("arbitrary",),
            vmem_limit_bytes=VMEM_LIMIT_V7X),
        name="out_ffn",
    )(*([x] * (D // LANES)), x, x, nc5, nc5, nc5, ns5, ns5, ns5, w_out, post_mix_g, pre_ffn_g, w_up,
      ffn_conv_w, ffn_conv_b, w_down, post_ffn_g)


def kernel(x_prompt, x_sample, pre_mix_g, w_in, conv_w, lam_re, lam_im, log_step, b_re, b_im,
           c_re, c_im, d_skip, w_glu, b_glu, gn_conv, gn_ssm, w_out, post_mix_g,
           pre_ffn_g, w_up, ffn_conv_w, ffn_conv_b, w_down, post_ffn_g):
    assert pre_mix_g.shape[0] == 1, "one encoder layer"
    wv, wy, coef = _s5_tables(lam_re[0], lam_im[0], log_step[0], b_re[0], b_im[0],
                              c_re[0], c_im[0])
    w_in_b = w_in[0].astype(BF16)
    w_out_b = w_out[0].astype(BF16)
    w_up_b = w_up[0].astype(BF16)
    w_down_b = w_down[0].astype(BF16)
    wglu_t = w_glu[0].T.astype(BF16)
    d_col = d_skip[0].reshape(D_SSM, 1)
    bglu_col = b_glu[0].reshape(D_SSM, 1)
    gn_ssm_col = gn_ssm[0].reshape(D_SSM, 1)

    def trunk(x):
        assert x.shape[1] % TILE == 0 and x.shape[2] == D_MODEL
        nconv, zut = _mixer_in(x, pre_mix_g, w_in_b, conv_w[0], gn_conv)
        nssm = _s5_mixer(zut, wv, wy, coef, d_col, wglu_t, bglu_col, gn_ssm_col)
        return _out_ffn(x, nconv, nssm, w_out_b, post_mix_g, pre_ffn_g, w_up_b,
                        ffn_conv_w[0], ffn_conv_b, w_down_b, post_ffn_g)

    return (trunk(x_prompt), trunk(x_sample))
```

```python
import functools

import jax
import jax.numpy as jnp
from jax import lax
from jax.experimental import pallas as pl
from jax.experimental.pallas import tpu as pltpu

D_MODEL = 1024
D_CONV = 512
D_SSM = 512
SSM_GROUP = 16
SSM_GROUPS = D_SSM // SSM_GROUP
SSM_STATE = 64
D_FF = 2816
EPS = 1e-6
LANES = 128

CHUNK = 16
TILE_CHUNKS = 128
TILE = CHUNK * TILE_CHUNKS
HALO = 16
FFN_CHUNKS = 32
FFN_TILE = CHUNK * FFN_CHUNKS
MXU_COLS = 512
FF_BLOCKS = tuple((c0, min(MXU_COLS, D_FF - c0)) for c0 in range(0, D_FF, MXU_COLS))
FF_BLOCK_MAX = MXU_COLS
FF_SPLIT = FF_BLOCKS[-1][0]
PIPE_LAG = 2
SCAN_PAD = 8
PAIR_UNROLL = 4
SCAN_UNROLL = 4
GLU_COLS = 512
VMEM_LIMIT_V7X = 56 * 1024 * 1024

F32 = jnp.float32
BF16 = jnp.bfloat16


def _rms(x, g):
    return x * lax.rsqrt(jnp.mean(x * x, axis=-1, keepdims=True) + EPS) * g


def _dot(a, b):
    return jnp.dot(a, b, preferred_element_type=F32)


def _interleave(heavy, light):
    i = j = 0
    while i < len(heavy) or j < len(light):
        if j >= len(light) or (i < len(heavy) and i * len(light) <= j * len(heavy)):
            heavy[i]()
            i += 1
        else:
            light[j]()
            j += 1


def _const_spec(shape):
    zeros = (0,) * len(shape)
    return pl.BlockSpec(shape, lambda *_: zeros, pipeline_mode=pl.Buffered(1))


def _s5_tables(lam_re, lam_im, log_step, b_re, b_im, c_re, c_im):
    T, G, P, H = CHUNK, SSM_GROUPS, SSM_STATE, SSM_GROUP
    lam = lax.complex(lam_re.astype(F32), lam_im.astype(F32))
    dt = jnp.exp(log_step.astype(F32))[..., None]
    lam_bar = jnp.exp(lam * dt)
    b_bar = ((lam_bar - 1.0) / lam)[..., None] * lax.complex(b_re.astype(F32), b_im.astype(F32))
    cmat = lax.complex(c_re.astype(F32), c_im.astype(F32))
    k = jnp.arange(T + 1, dtype=F32)
    pw = jnp.exp((lam * dt)[..., None] * k)

    kern = jnp.real(jnp.einsum('dgap,dgpk,dgph->dgkah', cmat, pw[..., :T], b_bar))
    tt = jnp.arange(T)
    lag = tt[:, None] - tt[None, :]
    kf = jnp.where((lag >= 0)[None, :, :, None, None], kern[0][:, jnp.clip(lag, 0, T - 1)], 0.0)
    kb = jnp.where((lag <= 0)[None, :, :, None, None], kern[1][:, jnp.clip(-lag, 0, T - 1)], 0.0)
    toe = (kf + kb).transpose(0, 1, 3, 2, 4).reshape(G, T * H, T * H)

    inc_f = pw[0][:, :, T - 1 - tt][..., None] * b_bar[0][:, :, None, :]
    inc_b = pw[1][:, :, tt][..., None] * b_bar[1][:, :, None, :]
    wv = jnp.concatenate([jnp.real(inc_f), jnp.imag(inc_f), jnp.real(inc_b), jnp.imag(inc_b)],
                         axis=1).reshape(G, 4 * P, T * H)

    st_f = cmat[0][:, None] * pw[0][:, :, 1 + tt].transpose(0, 2, 1)[:, :, None, :]
    st_b = cmat[1][:, None] * pw[1][:, :, T - tt].transpose(0, 2, 1)[:, :, None, :]
    mst = jnp.concatenate([jnp.real(st_f), -jnp.imag(st_f), jnp.real(st_b), -jnp.imag(st_b)],
                          axis=-1).reshape(G, T * H, 4 * P)
    wy = jnp.concatenate([toe, mst], axis=-1)

    a16 = pw[..., T].reshape(2, G // 2, 2 * P)
    coef = jnp.stack([jnp.real(a16[0]), jnp.imag(a16[0]), jnp.real(a16[1]), jnp.imag(a16[1])])
    return wv.astype(BF16), wy.astype(BF16), coef.astype(F32)


def _permuted_slab(xcols, t, n_chunks):
    return jnp.concatenate([xc[0, pl.ds(t, n_chunks, stride=CHUNK), :] for xc in xcols], axis=1)


def _column_block_specs(rows, index_map):
    return [pl.BlockSpec((1, rows, LANES), functools.partial(index_map, cb=cb))
            for cb in range(D_MODEL // LANES)]


def _mixer_in_kernel(*refs):
    ncb = D_MODEL // LANES
    xcols = refs[:ncb]
    xp_ref, xn_ref, g_ref, win_ref, cw_ref, gn_ref, nconv_ref, zut_ref, hn_scr = refs[ncb:]
    i = pl.program_id(1)
    last = pl.num_programs(1) - 1
    T, NC = CHUNK, TILE_CHUNKS
    g = g_ref[...]
    for t in range(T):
        xs = _permuted_slab(xcols, t, NC)
        hn_scr[t * NC:(t + 1) * NC, :] = _rms(xs, g).astype(BF16)
    xh = jnp.concatenate([xp_ref[0], xn_ref[0]], axis=0)
    row = lax.broadcasted_iota(jnp.int32, (HALO, 1), 0)
    inside = ((row < 8) & (i > 0)) | ((row >= 8) & (i < last))
    hn_scr[TILE:TILE + HALO, :] = jnp.where(inside, _rms(xh, g), 0.0).astype(BF16)

    z12 = _dot(hn_scr[...], win_ref[:, D_CONV:3 * D_CONV])
    p = z12[:, :D_CONV] * z12[:, D_CONV:]
    zb = _dot(hn_scr[0:TILE, :], win_ref[:, 0:D_CONV])
    w0, w1, w2 = cw_ref[0:1, :], cw_ref[1:2, :], cw_ref[2:3, :]
    gn = gn_ref[...]
    rid = lax.broadcasted_iota(jnp.int32, (NC, 1), 0)
    for t in range(T):
        cur = p[t * NC:(t + 1) * NC]
        if t > 0:
            prev = p[(t - 1) * NC:t * NC]
        else:
            prev = jnp.where(rid == 0, p[TILE + 7:TILE + 8],
                             pltpu.roll(p[(T - 1) * NC:T * NC], 1, 0))
        if t < T - 1:
            nxt = p[(t + 1) * NC:(t + 2) * NC]
        else:
            nxt = jnp.where(rid == NC - 1, p[TILE + 8:TILE + 9],
                            pltpu.roll(p[0:NC], NC - 1, 0))
        yc = zb[t * NC:(t + 1) * NC] * (w0 * prev + w1 * cur + w2 * nxt)
        nconv_ref[0, 0, t * NC:(t + 1) * NC, :] = _rms(yc, gn).astype(BF16)

    zu = _dot(hn_scr[0:TILE, :], win_ref[:, 3 * D_CONV:])
    zut_ref[0, 0] = zu.T.astype(BF16)


def _mixer_in(x, pre_mix_g, w_in, conv_w, gn_conv):
    B, L, D = x.shape
    nt = L // TILE
    blk8 = TILE // 8
    return pl.pallas_call(
        _mixer_in_kernel,
        grid=(B, nt),
        in_specs=_column_block_specs(TILE, lambda b, i, cb: (b, i, cb)) + [
            pl.BlockSpec((1, 8, D), lambda b, i: (b, jnp.maximum(i * blk8 - 1, 0), 0)),
            pl.BlockSpec((1, 8, D), lambda b, i: (b, jnp.minimum((i + 1) * blk8, L // 8 - 1), 0)),
            _const_spec((1, D)),
            _const_spec((D, 4 * D_CONV)),
            _const_spec((3, D_CONV)),
            _const_spec((1, D_CONV)),
        ],
        out_specs=[
            pl.BlockSpec((1, 1, TILE, D_CONV), lambda b, i: (b, i, 0, 0)),
            pl.BlockSpec((1, 1, D_SSM, TILE), lambda b, i: (b, i, 0, 0)),
        ],
        out_shape=[
            jax.ShapeDtypeStruct((B, nt, TILE, D_CONV), BF16),
            jax.ShapeDtypeStruct((B, nt, D_SSM, TILE), BF16),
        ],
        scratch_shapes=[pltpu.VMEM((TILE + HALO, D), BF16)],
        compiler_params=pltpu.CompilerParams(
            dimension_semantics=("parallel", "parallel"),
            vmem_limit_bytes=VMEM_LIMIT_V7X),
        name="mixer_in",
    )(*([x] * (D // LANES)), x, x, pre_mix_g, w_in, conv_w, gn_conv)


def _s5_kernel(zut_ref, wv_ref, wy_ref, coef_ref, d_ref, wglu_ref, bglu_ref, gn_ref,
               out_ref, vfr, vfi, vbr, vbi, yt_scr, *, nt):
    T, NC, H, P = CHUNK, TILE_CHUNKS, SSM_GROUP, SSM_STATE
    npair = SSM_GROUPS // 2
    nc = nt * NC
    pitch = nc + SCAN_PAD

    def chunk_inputs(i, grp):
        r0 = pl.multiple_of(grp * H, H)
        return jnp.concatenate(
            [zut_ref[0, i, pl.ds(r0, H), t * NC:(t + 1) * NC] for t in range(T)], axis=0)

    for i in range(nt):
        def inc_body(q, carry, i=i):
            va = _dot(wv_ref[2 * q], chunk_inputs(i, 2 * q))
            vb = _dot(wv_ref[2 * q + 1], chunk_inputs(i, 2 * q + 1))
            pieces = []
            for kind in range(4):
                pieces += [va[kind * P:(kind + 1) * P], vb[kind * P:(kind + 1) * P]]
            v = jnp.concatenate(pieces, axis=0).T
            r0 = pl.multiple_of(q * pitch + i * NC, 8)
            vfr[pl.ds(r0, NC), :] = v[:, 0:2 * P]
            vfi[pl.ds(r0, NC), :] = v[:, 2 * P:4 * P]
            vbr[pl.ds(r0, NC), :] = v[:, 4 * P:6 * P]
            vbi[pl.ds(r0, NC), :] = v[:, 6 * P:8 * P]
            return carry
        lax.fori_loop(0, npair, inc_body, 0, unroll=PAIR_UNROLL)

    afr, afi, abr, abi = coef_ref[0], coef_ref[1], coef_ref[2], coef_ref[3]

    def scan_body(k, carry):
        sfr, sfi, sbr, sbi = carry
        rows_f = pl.ds(k, npair, stride=pitch)
        rows_b = pl.ds(nc - 1 - k, npair, stride=pitch)
        ur, ui = vfr[rows_f, :], vfi[rows_f, :]
        wr, wi = vbr[rows_b, :], vbi[rows_b, :]
        vfr[rows_f, :] = sfr
        vfi[rows_f, :] = sfi
        vbr[rows_b, :] = sbr
        vbi[rows_b, :] = sbi
        return (afr * sfr - afi * sfi + ur, afr * sfi + afi * sfr + ui,
                abr * sbr - abi * sbi + wr, abr * sbi + abi * sbr + wi)

    zero = jnp.zeros((npair, 2 * P), F32)
    lax.fori_loop(0, nc, scan_body, (zero, zero, zero, zero), unroll=SCAN_UNROLL)

    for i in range(nt):
        def out_body(q, carry, i=i):
            r0 = pl.multiple_of(q * pitch + i * NC, 8)
            rows = pl.ds(r0, NC)
            sin = jnp.concatenate([vfr[rows, :], vfi[rows, :], vbr[rows, :], vbi[rows, :]],
                                  axis=1).T
            for j in range(2):
                grp = 2 * q + j
                st = jnp.concatenate([sin[(2 * kind + j) * P:(2 * kind + j + 1) * P]
                                      for kind in range(4)], axis=0)
                rhs = jnp.concatenate([chunk_inputs(i, grp), st.astype(BF16)], axis=0)
                yt = _dot(wy_ref[grp], rhs)
                h0 = pl.multiple_of(grp * H, H)
                for t in range(T):
                    yt_scr[pl.ds(h0, H), t * NC:(t + 1) * NC] = yt[t * H:(t + 1) * H]
            return carry
        lax.fori_loop(0, npair, out_body, 0, unroll=PAIR_UNROLL)

        for cb in range(TILE // GLU_COLS):
            cols = slice(cb * GLU_COLS, (cb + 1) * GLU_COLS)
            y = yt_scr[:, cols] + d_ref[...] * zut_ref[0, i, :, cols].astype(F32)
            y = jax.nn.gelu(y)
            gate = _dot(wglu_ref[...], y.astype(BF16)) + bglu_ref[...]
            y = y * jax.nn.sigmoid(gate)
            ms = jnp.mean(y * y, axis=0, keepdims=True)
            y = y * lax.rsqrt(ms + EPS) * gn_ref[...]
            out_ref[0, i, cols, :] = y.T.astype(BF16)


def _s5_mixer(zut, wv, wy, coef, d_col, wglu_t, bglu_col, gn_col):
    B, nt = zut.shape[0], zut.shape[1]
    npair = SSM_GROUPS // 2
    scan_rows = npair * (nt * TILE_CHUNKS + SCAN_PAD)
    return pl.pallas_call(
        functools.partial(_s5_kernel, nt=nt),
        grid=(B,),
        in_specs=[
            pl.BlockSpec((1, nt, D_SSM, TILE), lambda b: (b, 0, 0, 0)),
            _const_spec(wv.shape),
            _const_spec(wy.shape),
            _const_spec(coef.shape),
            _const_spec((D_SSM, 1)),
            _const_spec((D_SSM, D_SSM)),
            _const_spec((D_SSM, 1)),
            _const_spec((D_SSM, 1)),
        ],
        out_specs=pl.BlockSpec((1, nt, TILE, D_SSM), lambda b: (b, 0, 0, 0)),
        out_shape=jax.ShapeDtypeStruct((B, nt, TILE, D_SSM), BF16),
        scratch_shapes=[pltpu.VMEM((scan_rows, 2 * SSM_STATE), F32) for _ in range(4)]
        + [pltpu.VMEM((D_SSM, TILE), F32)],
        compiler_params=pltpu.CompilerParams(
            dimension_semantics=("parallel",),
            vmem_limit_bytes=VMEM_LIMIT_V7X),
        name="s5_mixer",
    )(zut, wv, wy, coef, d_col, wglu_t, bglu_col, gn_col)


def _slab_copies(hbm4, tile, nj, slabs, sems, to_hbm):
    b, c0 = tile // nj, (tile % nj) * FFN_CHUNKS
    copies = []
    for t in range(CHUNK):
        window = hbm4.at[b, pl.ds(c0, FFN_CHUNKS), t]
        src, dst = (slabs.at[t], window) if to_hbm else (window, slabs.at[t])
        copies.append(pltpu.make_async_copy(src, dst, sems.at[t]))
    return copies


def _out_ffn_kernel(x4_ref, xp_ref, xn_ref, nc_ref, ncp_ref, ncn_ref, ns_ref, nsp_ref, nsn_ref,
                    wout_ref, pmg_ref, pfg_ref, wup_ref, fcw_ref, fcb_ref, wdown_ref, pog_ref,
                    o4_ref, up_a, up_b, act_scr, x1_even, x1_odd, h2_even, h2_odd, f_scr,
                    mix_scr, xs_even, xs_odd, os_even, os_odd, x_sems, o_sems, *, nj):
    s = pl.program_id(0)
    nsteps = pl.num_programs(0)
    ntiles = nsteps - PIPE_LAG

    @pl.when(s == 0)
    def _():
        for ref in (x1_even, x1_odd, h2_even, h2_odd, f_scr):
            ref[...] = jnp.zeros(ref.shape, ref.dtype)
        for copy in _slab_copies(x4_ref, 0, nj, xs_even, x_sems.at[0], to_hbm=False):
            copy.start()

    shared = (xp_ref, xn_ref, nc_ref, ncp_ref, ncn_ref, ns_ref, nsp_ref, nsn_ref,
              wout_ref, pmg_ref, pfg_ref, wup_ref, fcw_ref, fcb_ref, wdown_ref, pog_ref,
              up_a, up_b, act_scr, f_scr, mix_scr)
    x1s, h2s = (x1_even, x1_odd), (h2_even, h2_odd)
    xss, oss = (xs_even, xs_odd), (os_even, os_odd)
    for parity in range(2):
        other = 1 - parity

        @pl.when(s % 2 == parity)
        def _(parity=parity, other=other):
            x_tile = jnp.minimum(s, ntiles - 1)
            for copy in _slab_copies(x4_ref, x_tile, nj, xss[parity], x_sems.at[parity], False):
                copy.wait()
            x_next = jnp.minimum(s + 1, ntiles - 1)
            for copy in _slab_copies(x4_ref, x_next, nj, xss[other], x_sems.at[other], False):
                copy.start()
            o_tile = jnp.where(s >= PIPE_LAG, s - PIPE_LAG, s)
            o_prev = jnp.where(s >= 2 * PIPE_LAG, s - 2 * PIPE_LAG, s - PIPE_LAG)

            @pl.when(s >= PIPE_LAG)
            def _():
                for copy in _slab_copies(o4_ref, o_prev, nj, oss[parity], o_sems.at[parity], True):
                    copy.wait()

            _out_ffn_step(shared, xss[parity], oss[parity], x1s[parity], h2s[parity], h2s[other],
                          s, ntiles, nj)
            for copy in _slab_copies(o4_ref, o_tile, nj, oss[parity], o_sems.at[parity], True):
                copy.start()

    @pl.when(s == nsteps - 1)
    def _():
        last_parity = (ntiles + PIPE_LAG - 1) % 2
        nxt = 1 - last_parity
        for copy in _slab_copies(x4_ref, ntiles - 1, nj, xss[nxt], x_sems.at[nxt], False):
            copy.wait()
        for par, tile in ((nxt, ntiles - 2), (last_parity, ntiles - 1)):
            for copy in _slab_copies(o4_ref, tile, nj, oss[par], o_sems.at[par], True):
                copy.wait()


def _out_ffn_step(shared, x_slabs, o_slabs, x1_tile, h2_new, h2_old, s, ntiles, nj):
    (xp_ref, xn_ref, nc_ref, ncp_ref, ncn_ref, ns_ref, nsp_ref, nsn_ref,
     wout_ref, pmg_ref, pfg_ref, wup_ref, fcw_ref, fcb_ref, wdown_ref, pog_ref,
     up_a, up_b, act_scr, f_scr, mix_scr) = shared
    j = jnp.minimum(s, ntiles - 1) % nj
    last = nj - 1
    T, NC, R = CHUNK, FFN_CHUNKS, FFN_TILE


    f_prev = f_scr[...]
    f_scale = lax.rsqrt(jnp.mean(f_prev * f_prev, axis=-1, keepdims=True) + EPS)

    def out_piece(t):
        rows = slice(t * NC, (t + 1) * NC)
        o_slabs[t] = x1_tile[rows, :] + f_scr[rows, :] * f_scale[rows] * pog_ref[...]

    out_stage = [functools.partial(out_piece, t) for t in range(T)]

    def with_halo(main_ref, prev_ref, next_ref):
        main = main_ref[0, 0].reshape(R, main_ref.shape[-1])
        halo = jnp.concatenate([prev_ref[0, 0, 0].astype(F32)[8:16],
                                next_ref[0, 0, 0].astype(F32)[0:8]], axis=0).astype(BF16)
        return jnp.concatenate([main, halo], axis=0)

    def proj_piece(c):
        cols = slice(c * MXU_COLS, (c + 1) * MXU_COLS)
        lhs = jnp.concatenate([with_halo(nc_ref, ncp_ref, ncn_ref),
                               with_halo(ns_ref, nsp_ref, nsn_ref)], axis=1)
        mix_scr[:, cols] = _dot(lhs, wout_ref[:, cols])

    def norm_piece(r):
        if r < T:
            rows = slice(r * NC, (r + 1) * NC)
            x1 = x_slabs[r] + _rms(mix_scr[rows, :], pmg_ref[...])
            x1_tile[rows, :] = x1
            h2_new[rows, :] = _rms(x1, pfg_ref[...]).astype(BF16)
        else:
            rows = slice(R, R + HALO)
            xr = jnp.concatenate([xp_ref[0], xn_ref[0]], axis=0)
            x1 = xr + _rms(mix_scr[rows, :], pmg_ref[...])
            row = lax.broadcasted_iota(jnp.int32, (HALO, 1), 0)
            inside = ((row < 8) & (j > 0)) | ((row >= 8) & (j < last))
            h2_new[rows, :] = jnp.where(inside, _rms(x1, pfg_ref[...]), 0.0).astype(BF16)

    proj_stage = [functools.partial(proj_piece, c) for c in range(D_MODEL // MXU_COLS)]
    norm_stage = [functools.partial(norm_piece, r) for r in range(T + 1)]

    rid = lax.broadcasted_iota(jnp.int32, (NC, 1), 0)

    def up_piece(k, part):
        c0, width = FF_BLOCKS[k]
        up = (up_a, up_b)[k % 2]
        res = _dot(h2_old[...], wup_ref[:, part * D_FF + c0:part * D_FF + c0 + width])
        nq = width // LANES
        for q in range(nq):
            up[part * nq + q] = res[:, q * LANES:(q + 1) * LANES]

    def conv_taps(up, blk, cols, t):
        cur = up[blk, t * NC:(t + 1) * NC, :]
        if t > 0:
            prev = up[blk, (t - 1) * NC:t * NC, :]
        else:
            prev = jnp.where(rid == 0, up[blk, R + 7:R + 8, :],
                             pltpu.roll(up[blk, (T - 1) * NC:T * NC, :], 1, 0))
        if t < T - 1:
            nxt = up[blk, (t + 1) * NC:(t + 2) * NC, :]
        else:
            nxt = jnp.where(rid == NC - 1, up[blk, R + 8:R + 9, :],
                            pltpu.roll(up[blk, 0:NC, :], NC - 1, 0))
        return (fcw_ref[0:1, cols] * prev + fcw_ref[1:2, cols] * cur + fcw_ref[2:3, cols] * nxt
                + fcb_ref[:, cols])

    def conv_piece(k, t):
        c0, width = FF_BLOCKS[k]
        up = (up_a, up_b)[k % 2]
        nq = width // LANES
        for q in range(nq):
            gcols = slice(c0 + q * LANES, c0 + (q + 1) * LANES)
            vcols = slice(D_FF + c0 + q * LANES, D_FF + c0 + (q + 1) * LANES)
            act = jax.nn.silu(conv_taps(up, q, gcols, t)) * conv_taps(up, nq + q, vcols, t)
            act_scr[t * NC:(t + 1) * NC, gcols] = act.astype(BF16)

    def down_piece(first, c):
        cols = slice(c * MXU_COLS, (c + 1) * MXU_COLS)
        if first:
            f_scr[:, cols] = _dot(act_scr[:, 0:FF_SPLIT], wdown_ref[0:FF_SPLIT, cols])
        else:
            f_scr[:, cols] += _dot(act_scr[:, FF_SPLIT:], wdown_ref[FF_SPLIT:, cols])

    nblk = len(FF_BLOCKS)
    up_stage = [[functools.partial(up_piece, k, part) for part in range(2)] for k in range(nblk)]
    conv_stage = [[functools.partial(conv_piece, k, t) for t in range(T)] for k in range(nblk)]
    down_stage = [[functools.partial(down_piece, first, c) for c in range(D_MODEL // MXU_COLS)]
                  for first in (True, False)]
    assert FF_SPLIT <= FF_BLOCKS[-1][0]

    _interleave(proj_stage, out_stage)
    _interleave(up_stage[0], norm_stage)
    for k in range(nblk):
        heavy = up_stage[k + 1] if k + 1 < nblk else down_stage[0]
        _interleave(heavy, conv_stage[k])
    _interleave(down_stage[1], [])


def _out_ffn(x, nconv, nssm, w_out, post_mix_g, pre_ffn_g, w_up, ffn_conv_w, ffn_conv_b,
             w_down, post_ffn_g):
    B, L, D = x.shape
    nt = L // TILE
    nj = L // FFN_TILE
    per_tile = TILE_CHUNKS // FFN_CHUNKS
    blk8 = FFN_TILE // 8
    nchunks = L // CHUNK
    nc5 = nconv.reshape(B, nt, CHUNK, TILE_CHUNKS, D_CONV)
    ns5 = nssm.reshape(B, nt, CHUNK, TILE_CHUNKS, D_SSM)

    ntiles = B * nj
    up_shape = (2 * FF_BLOCK_MAX // LANES, FFN_TILE + HALO, LANES)

    def in_tile(s):
        tile = jnp.minimum(s, ntiles - 1)
        return tile // nj, tile % nj

    def x_prev_map(s):
        b, j = in_tile(s)
        return (b, jnp.maximum(j * blk8 - 1, 0), 0)

    def x_next_map(s):
        b, j = in_tile(s)
        return (b, jnp.minimum((j + 1) * blk8, L // 8 - 1), 0)

    def main_map(s):
        b, j = in_tile(s)
        return (b, j // per_tile, 0, j % per_tile, 0)

    def prev_map(s):
        b, j = in_tile(s)
        c = jnp.maximum(j * FFN_CHUNKS - 1, 0)
        return (b, c // TILE_CHUNKS, CHUNK - 1, (c % TILE_CHUNKS) // 16, 0)

    def next_map(s):
        b, j = in_tile(s)
        c = jnp.minimum((j + 1) * FFN_CHUNKS, nchunks - 1)
        return (b, c // TILE_CHUNKS, 0, (c % TILE_CHUNKS) // 16, 0)

    act_specs = []
    for width in (D_CONV, D_SSM):
        act_specs += [pl.BlockSpec((1, 1, CHUNK, FFN_CHUNKS, width), main_map),
                      pl.BlockSpec((1, 1, 1, 16, width), prev_map),
                      pl.BlockSpec((1, 1, 1, 16, width), next_map)]
    x4 = x.reshape(B, nchunks, CHUNK, D)
    slab_shape = (CHUNK, FFN_CHUNKS, D)
    out4 = pl.pallas_call(
        functools.partial(_out_ffn_kernel, nj=nj),
        grid=(ntiles + PIPE_LAG,),
        in_specs=[
            pl.BlockSpec(memory_space=pl.ANY),
            pl.BlockSpec((1, 8, D), x_prev_map),
            pl.BlockSpec((1, 8, D), x_next_map),
        ] + act_specs + [
            _const_spec((D_CONV + D_SSM, D)),
            _const_spec((1, D)),
            _const_spec((1, D)),
            _const_spec((D, 2 * D_FF)),
            _const_spec((3, 2 * D_FF)),
            _const_spec((1, 2 * D_FF)),
            _const_spec((D_FF, D)),
            _const_spec((1, D)),
        ],
        out_specs=pl.BlockSpec(memory_space=pl.ANY),
        out_shape=jax.ShapeDtypeStruct((B, nchunks, CHUNK, D), F32),
        scratch_shapes=[pltpu.VMEM(up_shape, F32),
                        pltpu.VMEM(up_shape, F32),
                        pltpu.VMEM((FFN_TILE, D_FF), BF16),
                        pltpu.VMEM((FFN_TILE, D), F32),
                        pltpu.VMEM((FFN_TILE, D), F32),
                        pltpu.VMEM((FFN_TILE + HALO, D), BF16),
                        pltpu.VMEM((FFN_TILE + HALO, D), BF16),
                        pltpu.VMEM((FFN_TILE, D), F32),
                        pltpu.VMEM((FFN_TILE + HALO, D), F32),
                        pltpu.VMEM(slab_shape, F32),
                        pltpu.VMEM(slab_shape, F32),
                        pltpu.VMEM(slab_shape, F32),
                        pltpu.VMEM(slab_shape, F32),
                        pltpu.SemaphoreType.DMA((2, CHUNK)),
                        pltpu.SemaphoreType.DMA((2, CHUNK))],
        compiler_params=pltpu.CompilerParams(
            dimension_semantics=("arbitrary",),
            vmem_limit_bytes=VMEM_LIMIT_V7X),
        name="out_ffn",
    )(x4, x, x, nc5, nc5, nc5, ns5, ns5, ns5, w_out, post_mix_g, pre_ffn_g, w_up,
      ffn_conv_w, ffn_conv_b, w_down, post_ffn_g)
    return out4.reshape(B, L, D)


def kernel(x_prompt, x_sample, pre_mix_g, w_in, conv_w, lam_re, lam_im, log_step, b_re, b_im,
           c_re, c_im, d_skip, w_glu, b_glu, gn_conv, gn_ssm, w_out, post_mix_g,
           pre_ffn_g, w_up, ffn_conv_w, ffn_conv_b, w_down, post_ffn_g):
    assert pre_mix_g.shape[0] == 1, "one encoder layer"
    wv, wy, coef = _s5_tables(lam_re[0], lam_im[0], log_step[0], b_re[0], b_im[0],
                              c_re[0], c_im[0])
    w_in_b = w_in[0].astype(BF16)
    w_out_b = w_out[0].astype(BF16)
    w_up_b = w_up[0].astype(BF16)
    w_down_b = w_down[0].astype(BF16)
    wglu_t = w_glu[0].T.astype(BF16)
    d_col = d_skip[0].reshape(D_SSM, 1)
    bglu_col = b_glu[0].reshape(D_SSM, 1)
    gn_ssm_col = gn_ssm[0].reshape(D_SSM, 1)

    def trunk(x):
        assert x.shape[1] % TILE == 0 and x.shape[2] == D_MODEL
        nconv, zut = _mixer_in(x, pre_mix_g, w_in_b, conv_w[0], gn_conv)
        nssm = _s5_mixer(zut, wv, wy, coef, d_col, wglu_t, bglu_col, gn_ssm_col)
        return _out_ffn(x, nconv, nssm, w_out_b, post_mix_g, pre_ffn_g, w_up_b,
                        ffn_conv_w[0], ffn_conv_b, w_down_b, post_ffn_g)

    return (trunk(x_prompt), trunk(x_sample))
```

```python
import functools

import jax
import jax.numpy as jnp
from jax import lax
from jax.experimental import pallas as pl
from jax.experimental.pallas import tpu as pltpu

D_MODEL = 1024
D_CONV = 512
D_SSM = 512
SSM_GROUP = 16
SSM_GROUPS = D_SSM // SSM_GROUP
SSM_STATE = 64
D_FF = 2816
EPS = 1e-6
LANES = 128

CHUNK = 16
TILE_CHUNKS = 128
TILE = CHUNK * TILE_CHUNKS
PROJ_ROWS = 512
HALO = 16
FFN_CHUNKS = 32
FFN_TILE = CHUNK * FFN_CHUNKS
MXU_COLS = 512
FF_BLOCKS = tuple((c0, min(MXU_COLS, D_FF - c0)) for c0 in range(0, D_FF, MXU_COLS))
FF_BLOCK_MAX = MXU_COLS
FF_SPLIT = FF_BLOCKS[-1][0]
PIPE_LAG = 2
SCAN_PAD = 8
PAIR_UNROLL = 4
SCAN_UNROLL = 4
GLU_COLS = 512
VMEM_LIMIT_V7X = 56 * 1024 * 1024

F32 = jnp.float32
BF16 = jnp.bfloat16


def _rms(x, g):
    return x * lax.rsqrt(jnp.mean(x * x, axis=-1, keepdims=True) + EPS) * g


def _dot(a, b):
    return jnp.dot(a, b, preferred_element_type=F32)


def _gelu_tanh(x):
    c = 0.7978845608028654
    half_x = 0.5 * x
    return half_x + half_x * jnp.tanh(x * (c + (c * 0.044715) * (x * x)))


def _interleave(heavy, light):
    i = j = 0
    while i < len(heavy) or j < len(light):
        if j >= len(light) or (i < len(heavy) and i * len(light) <= j * len(heavy)):
            heavy[i]()
            i += 1
        else:
            light[j]()
            j += 1


def _const_spec(shape):
    zeros = (0,) * len(shape)
    return pl.BlockSpec(shape, lambda *_: zeros, pipeline_mode=pl.Buffered(1))


def _s5_tables(lam_re, lam_im, log_step, b_re, b_im, c_re, c_im, d_skip):
    T, G, P, H = CHUNK, SSM_GROUPS, SSM_STATE, SSM_GROUP
    lam = lax.complex(lam_re.astype(F32), lam_im.astype(F32))
    dt = jnp.exp(log_step.astype(F32))[..., None]
    lam_bar = jnp.exp(lam * dt)
    b_bar = ((lam_bar - 1.0) / lam)[..., None] * lax.complex(b_re.astype(F32), b_im.astype(F32))
    cmat = lax.complex(c_re.astype(F32), c_im.astype(F32))
    k = jnp.arange(T + 1, dtype=F32)
    pw = jnp.exp((lam * dt)[..., None] * k)

    kern = jnp.real(jnp.einsum('dgap,dgpk,dgph->dgkah', cmat, pw[..., :T], b_bar))
    tt = jnp.arange(T)
    lag = tt[:, None] - tt[None, :]
    kf = jnp.where((lag >= 0)[None, :, :, None, None], kern[0][:, jnp.clip(lag, 0, T - 1)], 0.0)
    kb = jnp.where((lag <= 0)[None, :, :, None, None], kern[1][:, jnp.clip(-lag, 0, T - 1)], 0.0)
    toe = (kf + kb).transpose(0, 1, 3, 2, 4).reshape(G, T * H, T * H)
    skip = jnp.tile(d_skip.astype(F32).reshape(G, 1, H), (1, T, 1)).reshape(G, T * H)
    toe = toe + skip[:, :, None] * jnp.eye(T * H, dtype=F32)[None]

    inc_f = pw[0][:, :, T - 1 - tt][..., None] * b_bar[0][:, :, None, :]
    inc_b = pw[1][:, :, tt][..., None] * b_bar[1][:, :, None, :]
    wv = jnp.concatenate([jnp.real(inc_f), jnp.imag(inc_f), jnp.real(inc_b), jnp.imag(inc_b)],
                         axis=1).reshape(G, 4 * P, T * H)

    st_f = cmat[0][:, None] * pw[0][:, :, 1 + tt].transpose(0, 2, 1)[:, :, None, :]
    st_b = cmat[1][:, None] * pw[1][:, :, T - tt].transpose(0, 2, 1)[:, :, None, :]
    mst = jnp.concatenate([jnp.real(st_f), -jnp.imag(st_f), jnp.real(st_b), -jnp.imag(st_b)],
                          axis=-1).reshape(G, T * H, 4 * P)
    wy = jnp.concatenate([toe, mst], axis=-1)

    a16 = pw[..., T].reshape(2, G // 2, 2 * P)
    coef = jnp.stack([jnp.real(a16[0]), jnp.imag(a16[0]), jnp.real(a16[1]), jnp.imag(a16[1])])
    return wv.astype(BF16), wy.astype(BF16), coef.astype(F32)


def _slab_copies(hbm4, b, c0, n_chunks, slabs, sems, to_hbm):
    copies = []
    for t in range(CHUNK):
        window = hbm4.at[b, pl.ds(c0, n_chunks), t]
        src, dst = (slabs.at[t], window) if to_hbm else (window, slabs.at[t])
        copies.append(pltpu.make_async_copy(src, dst, sems.at[t]))
    return copies


def _mixer_in_kernel(x4_ref, xp_ref, xn_ref, g_ref, win_ref, cw_ref, gn_ref,
                     nconv_ref, zut_ref, hn_even, hn_odd, xs_even, xs_odd, p_scr, zb_scr,
                     x_sems, *, nt):
    s = pl.program_id(0)
    nsteps = pl.num_programs(0)
    ntiles = nsteps - 1

    def x_copies(tile, slabs, sems):
        return _slab_copies(x4_ref, tile // nt, (tile % nt) * TILE_CHUNKS, TILE_CHUNKS,
                            slabs, sems, to_hbm=False)

    @pl.when(s == 0)
    def _():
        for ref in (hn_even, hn_odd):
            ref[...] = jnp.zeros(ref.shape, ref.dtype)
        for copy in x_copies(0, xs_even, x_sems.at[0]):
            copy.start()

    hns, xss = (hn_even, hn_odd), (xs_even, xs_odd)
    for parity in range(2):
        other = 1 - parity

        @pl.when(s % 2 == parity)
        def _(parity=parity, other=other):
            tile = jnp.minimum(s, ntiles - 1)
            for copy in x_copies(tile, xss[parity], x_sems.at[parity]):
                copy.wait()
            for copy in x_copies(jnp.minimum(s + 1, ntiles - 1), xss[other], x_sems.at[other]):
                copy.start()
            _mixer_in_step(xss[parity], xp_ref, xn_ref, g_ref, win_ref, cw_ref, gn_ref,
                           nconv_ref, zut_ref, hns[parity], hns[other], p_scr, zb_scr,
                           tile % nt, nt)

    @pl.when(s == nsteps - 1)
    def _():
        nxt = 1 - (nsteps - 1) % 2
        for copy in x_copies(ntiles - 1, xss[nxt], x_sems.at[nxt]):
            copy.wait()


def _mixer_in_step(x_slabs, xp_ref, xn_ref, g_ref, win_ref, cw_ref, gn_ref, nconv_ref, zut_ref,
                   hn_new, hn_old, p_scr, zb_scr, i, nt):
    T, NC = CHUNK, TILE_CHUNKS
    half = D_CONV // 2

    def norm_piece(t):
        if t < T:
            hn_new[t * NC:(t + 1) * NC, :] = _rms(x_slabs[t], g_ref[...]).astype(BF16)
        else:
            xh = jnp.concatenate([xp_ref[0], xn_ref[0]], axis=0)
            row = lax.broadcasted_iota(jnp.int32, (HALO, 1), 0)
            inside = ((row < 8) & (i > 0)) | ((row >= 8) & (i < nt - 1))
            hn_new[TILE:TILE + HALO, :] = jnp.where(inside, _rms(xh, g_ref[...]), 0.0).astype(BF16)

    norm_stage = [functools.partial(norm_piece, t) for t in range(T + 1)]

    nblk = TILE // PROJ_ROWS

    def block_rows(r, with_halo):
        end = TILE + HALO if (with_halo and r == nblk - 1) else (r + 1) * PROJ_ROWS
        return slice(r * PROJ_ROWS, end)

    def gate_piece(r, k):
        rows = block_rows(r, True)
        zc = _dot(hn_old[rows, :], win_ref[:, D_CONV + k * half:D_CONV + (k + 1) * half])
        zx = _dot(hn_old[rows, :], win_ref[:, 2 * D_CONV + k * half:2 * D_CONV + (k + 1) * half])
        p_scr[rows, k * half:(k + 1) * half] = zc * zx

    def zb_piece(r):
        rows = block_rows(r, False)
        zb_scr[rows, :] = _dot(hn_old[rows, :], win_ref[:, 0:D_CONV])

    def zu_piece(r):
        rows = block_rows(r, False)
        zu = _dot(hn_new[rows, :], win_ref[:, 3 * D_CONV:])
        zut_ref[0, 0, :, rows] = zu.T.astype(BF16)

    rid = lax.broadcasted_iota(jnp.int32, (NC, 1), 0)

    def conv_piece(t):
        cur = p_scr[t * NC:(t + 1) * NC, :]
        if t > 0:
            prev = p_scr[(t - 1) * NC:t * NC, :]
        else:
            prev = jnp.where(rid == 0, p_scr[TILE + 7:TILE + 8, :],
                             pltpu.roll(p_scr[(T - 1) * NC:T * NC, :], 1, 0))
        if t < T - 1:
            nxt = p_scr[(t + 1) * NC:(t + 2) * NC, :]
        else:
            nxt = jnp.where(rid == NC - 1, p_scr[TILE + 8:TILE + 9, :],
                            pltpu.roll(p_scr[0:NC, :], NC - 1, 0))
        yc = zb_scr[t * NC:(t + 1) * NC, :] * (cw_ref[0:1, :] * prev + cw_ref[1:2, :] * cur
                                               + cw_ref[2:3, :] * nxt)
        nconv_ref[0, 0, t * NC:(t + 1) * NC, :] = _rms(yc, gn_ref[...]).astype(BF16)

    gate_stage = [functools.partial(gate_piece, r, k) for r in range(nblk) for k in range(2)]
    zb_stage = [functools.partial(zb_piece, r) for r in range(nblk)]
    zu_stage = [functools.partial(zu_piece, r) for r in range(nblk)]
    conv_stage = [functools.partial(conv_piece, t) for t in range(T)]
    _interleave(gate_stage + zb_stage, norm_stage)
    _interleave(zu_stage, conv_stage)


def _mixer_in(x, pre_mix_g, w_in, conv_w, gn_conv):
    B, L, D = x.shape
    nt = L // TILE
    blk8 = TILE // 8
    ntiles = B * nt

    def in_tile(s):
        tile = jnp.minimum(s, ntiles - 1)
        return tile // nt, tile % nt

    def out_map(s):
        tile = jnp.maximum(s - 1, 0)
        return (tile // nt, tile % nt, 0, 0)

    def x_prev_map(s):
        b, i = in_tile(s)
        return (b, jnp.maximum(i * blk8 - 1, 0), 0)

    def x_next_map(s):
        b, i = in_tile(s)
        return (b, jnp.minimum((i + 1) * blk8, L // 8 - 1), 0)

    x4 = x.reshape(B, L // CHUNK, CHUNK, D)
    slab_shape = (CHUNK, TILE_CHUNKS, D)
    return pl.pallas_call(
        functools.partial(_mixer_in_kernel, nt=nt),
        grid=(ntiles + 1,),
        in_specs=[
            pl.BlockSpec(memory_space=pl.ANY),
            pl.BlockSpec((1, 8, D), x_prev_map),
            pl.BlockSpec((1, 8, D), x_next_map),
            _const_spec((1, D)),
            _const_spec((D, 4 * D_CONV)),
            _const_spec((3, D_CONV)),
            _const_spec((1, D_CONV)),
        ],
        out_specs=[
            pl.BlockSpec((1, 1, TILE, D_CONV), out_map),
            pl.BlockSpec((1, 1, D_SSM, TILE), lambda s: in_tile(s) + (0, 0)),
        ],
        out_shape=[
            jax.ShapeDtypeStruct((B, nt, TILE, D_CONV), BF16),
            jax.ShapeDtypeStruct((B, nt, D_SSM, TILE), BF16),
        ],
        scratch_shapes=[pltpu.VMEM((TILE + HALO, D), BF16),
                        pltpu.VMEM((TILE + HALO, D), BF16),
                        pltpu.VMEM(slab_shape, F32),
                        pltpu.VMEM(slab_shape, F32),
                        pltpu.VMEM((TILE + HALO, D_CONV), F32),
                        pltpu.VMEM((TILE, D_CONV), F32),
                        pltpu.SemaphoreType.DMA((2, CHUNK))],
        compiler_params=pltpu.CompilerParams(
            dimension_semantics=("arbitrary",),
            vmem_limit_bytes=VMEM_LIMIT_V7X),
        name="mixer_in",
    )(x4, x, x, pre_mix_g, w_in, conv_w, gn_conv)


def _s5_kernel(zut_ref, wv_ref, wy_ref, coef_ref, wglu_ref, bglu_ref, gn_ref,
               out_ref, vfr, vfi, vbr, vbi, yt_scr, *, nt):
    T, NC, H, P = CHUNK, TILE_CHUNKS, SSM_GROUP, SSM_STATE
    npair = SSM_GROUPS // 2
    nc = nt * NC
    pitch = nc + SCAN_PAD

    def chunk_inputs(i, grp):
        r0 = pl.multiple_of(grp * H, H)
        return jnp.concatenate(
            [zut_ref[0, i, pl.ds(r0, H), t * NC:(t + 1) * NC] for t in range(T)], axis=0)

    for i in range(nt):
        def inc_body(q, carry, i=i):
            va = _dot(wv_ref[2 * q], chunk_inputs(i, 2 * q))
            vb = _dot(wv_ref[2 * q + 1], chunk_inputs(i, 2 * q + 1))
            pieces = []
            for kind in range(4):
                pieces += [va[kind * P:(kind + 1) * P], vb[kind * P:(kind + 1) * P]]
            v = jnp.concatenate(pieces, axis=0).T
            r0 = pl.multiple_of(q * pitch + i * NC, 8)
            vfr[pl.ds(r0, NC), :] = v[:, 0:2 * P]
            vfi[pl.ds(r0, NC), :] = v[:, 2 * P:4 * P]
            vbr[pl.ds(r0, NC), :] = v[:, 4 * P:6 * P]
            vbi[pl.ds(r0, NC), :] = v[:, 6 * P:8 * P]
            return carry
        lax.fori_loop(0, npair, inc_body, 0, unroll=PAIR_UNROLL)

    afr, afi, abr, abi = coef_ref[0], coef_ref[1], coef_ref[2], coef_ref[3]

    def scan_body(k, carry):
        sfr, sfi, sbr, sbi = carry
        rows_f = pl.ds(k, npair, stride=pitch)
        rows_b = pl.ds(nc - 1 - k, npair, stride=pitch)
        ur, ui = vfr[rows_f, :], vfi[rows_f, :]
        wr, wi = vbr[rows_b, :], vbi[rows_b, :]
        vfr[rows_f, :] = sfr
        vfi[rows_f, :] = sfi
        vbr[rows_b, :] = sbr
        vbi[rows_b, :] = sbi
        return (afr * sfr - afi * sfi + ur, afr * sfi + afi * sfr + ui,
                abr * sbr - abi * sbi + wr, abr * sbi + abi * sbr + wi)

    zero = jnp.zeros((npair, 2 * P), F32)
    lax.fori_loop(0, nc, scan_body, (zero, zero, zero, zero), unroll=SCAN_UNROLL)

    for i in range(nt):
        def out_body(q, carry, i=i):
            r0 = pl.multiple_of(q * pitch + i * NC, 8)
            rows = pl.ds(r0, NC)
            sin = jnp.concatenate([vfr[rows, :], vfi[rows, :], vbr[rows, :], vbi[rows, :]],
                                  axis=1).T
            for j in range(2):
                grp = 2 * q + j
                st = jnp.concatenate([sin[(2 * kind + j) * P:(2 * kind + j + 1) * P]
                                      for kind in range(4)], axis=0)
                rhs = jnp.concatenate([chunk_inputs(i, grp), st.astype(BF16)], axis=0)
                yt = _dot(wy_ref[grp], rhs)
                h0 = pl.multiple_of(grp * H, H)
                for t in range(T):
                    yt_scr[pl.ds(h0, H), t * NC:(t + 1) * NC] = yt[t * H:(t + 1) * H]
            return carry
        lax.fori_loop(0, npair, out_body, 0, unroll=PAIR_UNROLL)

        for cb in range(TILE // GLU_COLS):
            cols = slice(cb * GLU_COLS, (cb + 1) * GLU_COLS)
            y = _gelu_tanh(yt_scr[:, cols])
            gate = _dot(wglu_ref[...], y.astype(BF16)) + bglu_ref[...]
            y = y * jax.nn.sigmoid(gate)
            ms = jnp.mean(y * y, axis=0, keepdims=True)
            y = y * lax.rsqrt(ms + EPS) * gn_ref[...]
            out_ref[0, i, cols, :] = y.T.astype(BF16)


def _s5_mixer(zut, wv, wy, coef, wglu_t, bglu_col, gn_col):
    B, nt = zut.shape[0], zut.shape[1]
    npair = SSM_GROUPS // 2
    scan_rows = npair * (nt * TILE_CHUNKS + SCAN_PAD)
    return pl.pallas_call(
        functools.partial(_s5_kernel, nt=nt),
        grid=(B,),
        in_specs=[
            pl.BlockSpec((1, nt, D_SSM, TILE), lambda b: (b, 0, 0, 0)),
            _const_spec(wv.shape),
            _const_spec(wy.shape),
            _const_spec(coef.shape),
            _const_spec((D_SSM, D_SSM)),
            _const_spec((D_SSM, 1)),
            _const_spec((D_SSM, 1)),
        ],
        out_specs=pl.BlockSpec((1, nt, TILE, D_SSM), lambda b: (b, 0, 0, 0)),
        out_shape=jax.ShapeDtypeStruct((B, nt, TILE, D_SSM), BF16),
        scratch_shapes=[pltpu.VMEM((scan_rows, 2 * SSM_STATE), F32) for _ in range(4)]
        + [pltpu.VMEM((D_SSM, TILE), F32)],
        compiler_params=pltpu.CompilerParams(
            dimension_semantics=("parallel",),
            vmem_limit_bytes=VMEM_LIMIT_V7X),
        name="s5_mixer",
    )(zut, wv, wy, coef, wglu_t, bglu_col, gn_col)


def _ffn_slab_copies(hbm4, tile, nj, slabs, sems, to_hbm):
    return _slab_copies(hbm4, tile // nj, (tile % nj) * FFN_CHUNKS, FFN_CHUNKS, slabs, sems, to_hbm)


def _out_ffn_kernel(x4_ref, xp_ref, xn_ref, nc_ref, ncp_ref, ncn_ref, ns_ref, nsp_ref, nsn_ref,
                    wout_ref, pmg_ref, pfg_ref, wup_ref, fcw_ref, fcb_ref, wdown_ref, pog_ref,
                    o4_ref, up_a, up_b, act_scr, x1_even, x1_odd, h2_even, h2_odd, f_scr,
                    mix_scr, xs_even, xs_odd, os_even, os_odd, x_sems, o_sems, *, nj):
    s = pl.program_id(0)
    nsteps = pl.num_programs(0)
    ntiles = nsteps - PIPE_LAG

    @pl.when(s == 0)
    def _():
        for ref in (x1_even, x1_odd, h2_even, h2_odd, f_scr):
            ref[...] = jnp.zeros(ref.shape, ref.dtype)
        for copy in _ffn_slab_copies(x4_ref, 0, nj, xs_even, x_sems.at[0], to_hbm=False):
            copy.start()

    shared = (xp_ref, xn_ref, nc_ref, ncp_ref, ncn_ref, ns_ref, nsp_ref, nsn_ref,
              wout_ref, pmg_ref, pfg_ref, wup_ref, fcw_ref, fcb_ref, wdown_ref, pog_ref,
              up_a, up_b, act_scr, f_scr, mix_scr)
    x1s, h2s = (x1_even, x1_odd), (h2_even, h2_odd)
    xss, oss = (xs_even, xs_odd), (os_even, os_odd)
    for parity in range(2):
        other = 1 - parity

        @pl.when(s % 2 == parity)
        def _(parity=parity, other=other):
            x_tile = jnp.minimum(s, ntiles - 1)
            for copy in _ffn_slab_copies(x4_ref, x_tile, nj, xss[parity], x_sems.at[parity], False):
                copy.wait()
            x_next = jnp.minimum(s + 1, ntiles - 1)
            for copy in _ffn_slab_copies(x4_ref, x_next, nj, xss[other], x_sems.at[other], False):
                copy.start()
            o_tile = jnp.where(s >= PIPE_LAG, s - PIPE_LAG, s)
            o_prev = jnp.where(s >= 2 * PIPE_LAG, s - 2 * PIPE_LAG, s - PIPE_LAG)

            @pl.when(s >= PIPE_LAG)
            def _():
                for copy in _ffn_slab_copies(o4_ref, o_prev, nj, oss[parity], o_sems.at[parity], True):
                    copy.wait()

            _out_ffn_step(shared, xss[parity], oss[parity], x1s[parity], h2s[parity], h2s[other],
                          s, ntiles, nj)
            for copy in _ffn_slab_copies(o4_ref, o_tile, nj, oss[parity], o_sems.at[parity], True):
                copy.start()

    @pl.when(s == nsteps - 1)
    def _():
        last_parity = (ntiles + PIPE_LAG - 1) % 2
        nxt = 1 - last_parity
        for copy in _ffn_slab_copies(x4_ref, ntiles - 1, nj, xss[nxt], x_sems.at[nxt], False):
            copy.wait()
        for par, tile in ((nxt, ntiles - 2), (last_parity, ntiles - 1)):
            for copy in _ffn_slab_copies(o4_ref, tile, nj, oss[par], o_sems.at[par], True):
                copy.wait()


def _out_ffn_step(shared, x_slabs, o_slabs, x1_tile, h2_new, h2_old, s, ntiles, nj):
    (xp_ref, xn_ref, nc_ref, ncp_ref, ncn_ref, ns_ref, nsp_ref, nsn_ref,
     wout_ref, pmg_ref, pfg_ref, wup_ref, fcw_ref, fcb_ref, wdown_ref, pog_ref,
     up_a, up_b, act_scr, f_scr, mix_scr) = shared
    j = jnp.minimum(s, ntiles - 1) % nj
    last = nj - 1
    T, NC, R = CHUNK, FFN_CHUNKS, FFN_TILE


    f_prev = f_scr[...]
    f_scale = lax.rsqrt(jnp.mean(f_prev * f_prev, axis=-1, keepdims=True) + EPS)

    def out_piece(t):
        rows = slice(t * NC, (t + 1) * NC)
        o_slabs[t] = x1_tile[rows, :] + f_scr[rows, :] * f_scale[rows] * pog_ref[...]

    out_stage = [functools.partial(out_piece, t) for t in range(T)]

    def with_halo(main_ref, prev_ref, next_ref):
        main = main_ref[0, 0].reshape(R, main_ref.shape[-1])
        halo = jnp.concatenate([prev_ref[0, 0, 0].astype(F32)[8:16],
                                next_ref[0, 0, 0].astype(F32)[0:8]], axis=0).astype(BF16)
        return jnp.concatenate([main, halo], axis=0)

    def proj_piece(c):
        cols = slice(c * MXU_COLS, (c + 1) * MXU_COLS)
        lhs = jnp.concatenate([with_halo(nc_ref, ncp_ref, ncn_ref),
                               with_halo(ns_ref, nsp_ref, nsn_ref)], axis=1)
        mix_scr[:, cols] = _dot(lhs, wout_ref[:, cols])

    def norm_piece(r):
        if r < T:
            rows = slice(r * NC, (r + 1) * NC)
            x1 = x_slabs[r] + _rms(mix_scr[rows, :], pmg_ref[...])
            x1_tile[rows, :] = x1
            h2_new[rows, :] = _rms(x1, pfg_ref[...]).astype(BF16)
        else:
            rows = slice(R, R + HALO)
            xr = jnp.concatenate([xp_ref[0], xn_ref[0]], axis=0)
            x1 = xr + _rms(mix_scr[rows, :], pmg_ref[...])
            row = lax.broadcasted_iota(jnp.int32, (HALO, 1), 0)
            inside = ((row < 8) & (j > 0)) | ((row >= 8) & (j < last))
            h2_new[rows, :] = jnp.where(inside, _rms(x1, pfg_ref[...]), 0.0).astype(BF16)

    proj_stage = [functools.partial(proj_piece, c) for c in range(D_MODEL // MXU_COLS)]
    norm_stage = [functools.partial(norm_piece, r) for r in range(T + 1)]

    rid = lax.broadcasted_iota(jnp.int32, (NC, 1), 0)

    def up_piece(k, part):
        c0, width = FF_BLOCKS[k]
        up = (up_a, up_b)[k % 2]
        res = _dot(h2_old[...], wup_ref[:, part * D_FF + c0:part * D_FF + c0 + width])
        nq = width // LANES
        for q in range(nq):
            up[part * nq + q] = res[:, q * LANES:(q + 1) * LANES]

    def conv_taps(up, blk, cols, t):
        cur = up[blk, t * NC:(t + 1) * NC, :]
        if t > 0:
            prev = up[blk, (t - 1) * NC:t * NC, :]
        else:
            prev = jnp.where(rid == 0, up[blk, R + 7:R + 8, :],
                             pltpu.roll(up[blk, (T - 1) * NC:T * NC, :], 1, 0))
        if t < T - 1:
            nxt = up[blk, (t + 1) * NC:(t + 2) * NC, :]
        else:
            nxt = jnp.where(rid == NC - 1, up[blk, R + 8:R + 9, :],
                            pltpu.roll(up[blk, 0:NC, :], NC - 1, 0))
        return (fcw_ref[0:1, cols] * prev + fcw_ref[1:2, cols] * cur + fcw_ref[2:3, cols] * nxt
                + fcb_ref[:, cols])

    def conv_piece(k, t):
        c0, width = FF_BLOCKS[k]
        up = (up_a, up_b)[k % 2]
        nq = width // LANES
        for q in range(nq):
            gcols = slice(c0 + q * LANES, c0 + (q + 1) * LANES)
            vcols = slice(D_FF + c0 + q * LANES, D_FF + c0 + (q + 1) * LANES)
            act = jax.nn.silu(conv_taps(up, q, gcols, t)) * conv_taps(up, nq + q, vcols, t)
            act_scr[t * NC:(t + 1) * NC, gcols] = act.astype(BF16)

    def down_piece(first, c):
        cols = slice(c * MXU_COLS, (c + 1) * MXU_COLS)
        if first:
            f_scr[:, cols] = _dot(act_scr[:, 0:FF_SPLIT], wdown_ref[0:FF_SPLIT, cols])
        else:
            f_scr[:, cols] += _dot(act_scr[:, FF_SPLIT:], wdown_ref[FF_SPLIT:, cols])

    nblk = len(FF_BLOCKS)
    up_stage = [[functools.partial(up_piece, k, part) for part in range(2)] for k in range(nblk)]
    conv_stage = [[functools.partial(conv_piece, k, t) for t in range(T)] for k in range(nblk)]
    down_stage = [[functools.partial(down_piece, first, c) for c in range(D_MODEL // MXU_COLS)]
                  for first in (True, False)]
    assert FF_SPLIT <= FF_BLOCKS[-1][0]

    _interleave(proj_stage, out_stage)
    _interleave(up_stage[0], norm_stage)
    for k in range(nblk):
        heavy = up_stage[k + 1] if k + 1 < nblk else down_stage[0]
        _interleave(heavy, conv_stage[k])
    _interleave(down_stage[1], [])


def _out_ffn(x, nconv, nssm, w_out, post_mix_g, pre_ffn_g, w_up, ffn_conv_w, ffn_conv_b,
             w_down, post_ffn_g):
    B, L, D = x.shape
    nt = L // TILE
    nj = L // FFN_TILE
    per_tile = TILE_CHUNKS // FFN_CHUNKS
    blk8 = FFN_TILE // 8
    nchunks = L // CHUNK
    nc5 = nconv.reshape(B, nt, CHUNK, TILE_CHUNKS, D_CONV)
    ns5 = nssm.reshape(B, nt, CHUNK, TILE_CHUNKS, D_SSM)

    ntiles = B * nj
    up_shape = (2 * FF_BLOCK_MAX // LANES, FFN_TILE + HALO, LANES)

    def in_tile(s):
        tile = jnp.minimum(s, ntiles - 1)
        return tile // nj, tile % nj

    def x_prev_map(s):
        b, j = in_tile(s)
        return (b, jnp.maximum(j * blk8 - 1, 0), 0)

    def x_next_map(s):
        b, j = in_tile(s)
        return (b, jnp.minimum((j + 1) * blk8, L // 8 - 1), 0)

    def main_map(s):
        b, j = in_tile(s)
        return (b, j // per_tile, 0, j % per_tile, 0)

    def prev_map(s):
        b, j = in_tile(s)
        c = jnp.maximum(j * FFN_CHUNKS - 1, 0)
        return (b, c // TILE_CHUNKS, CHUNK - 1, (c % TILE_CHUNKS) // 16, 0)

    def next_map(s):
        b, j = in_tile(s)
        c = jnp.minimum((j + 1) * FFN_CHUNKS, nchunks - 1)
        return (b, c // TILE_CHUNKS, 0, (c % TILE_CHUNKS) // 16, 0)

    act_specs = []
    for width in (D_CONV, D_SSM):
        act_specs += [pl.BlockSpec((1, 1, CHUNK, FFN_CHUNKS, width), main_map),
                      pl.BlockSpec((1, 1, 1, 16, width), prev_map),
                      pl.BlockSpec((1, 1, 1, 16, width), next_map)]
    x4 = x.reshape(B, nchunks, CHUNK, D)
    slab_shape = (CHUNK, FFN_CHUNKS, D)
    out4 = pl.pallas_call(
        functools.partial(_out_ffn_kernel, nj=nj),
        grid=(ntiles + PIPE_LAG,),
        in_specs=[
            pl.BlockSpec(memory_space=pl.ANY),
            pl.BlockSpec((1, 8, D), x_prev_map),
            pl.BlockSpec((1, 8, D), x_next_map),
        ] + act_specs + [
            _const_spec((D_CONV + D_SSM, D)),
            _const_spec((1, D)),
            _const_spec((1, D)),
            _const_spec((D, 2 * D_FF)),
            _const_spec((3, 2 * D_FF)),
            _const_spec((1, 2 * D_FF)),
            _const_spec((D_FF, D)),
            _const_spec((1, D)),
        ],
        out_specs=pl.BlockSpec(memory_space=pl.ANY),
        out_shape=jax.ShapeDtypeStruct((B, nchunks, CHUNK, D), F32),
        scratch_shapes=[pltpu.VMEM(up_shape, F32),
                        pltpu.VMEM(up_shape, F32),
                        pltpu.VMEM((FFN_TILE, D_FF), BF16),
                        pltpu.VMEM((FFN_TILE, D), F32),
                        pltpu.VMEM((FFN_TILE, D), F32),
                        pltpu.VMEM((FFN_TILE + HALO, D), BF16),
                        pltpu.VMEM((FFN_TILE + HALO, D), BF16),
                        pltpu.VMEM((FFN_TILE, D), F32),
                        pltpu.VMEM((FFN_TILE + HALO, D), F32),
                        pltpu.VMEM(slab_shape, F32),
                        pltpu.VMEM(slab_shape, F32),
                        pltpu.VMEM(slab_shape, F32),
                        pltpu.VMEM(slab_shape, F32),
                        pltpu.SemaphoreType.DMA((2, CHUNK)),
                        pltpu.SemaphoreType.DMA((2, CHUNK))],
        compiler_params=pltpu.CompilerParams(
            dimension_semantics=("arbitrary",),
            vmem_limit_bytes=VMEM_LIMIT_V7X),
        name="out_ffn",
    )(x4, x, x, nc5, nc5, nc5, ns5, ns5, ns5, w_out, post_mix_g, pre_ffn_g, w_up,
      ffn_conv_w, ffn_conv_b, w_down, post_ffn_g)
    return out4.reshape(B, L, D)


def kernel(x_prompt, x_sample, pre_mix_g, w_in, conv_w, lam_re, lam_im, log_step, b_re, b_im,
           c_re, c_im, d_skip, w_glu, b_glu, gn_conv, gn_ssm, w_out, post_mix_g,
           pre_ffn_g, w_up, ffn_conv_w, ffn_conv_b, w_down, post_ffn_g):
    assert pre_mix_g.shape[0] == 1, "one encoder layer"
    wv, wy, coef = _s5_tables(lam_re[0], lam_im[0], log_step[0], b_re[0], b_im[0],
                              c_re[0], c_im[0], d_skip[0])
    w_in_b = w_in[0].astype(BF16)
    w_out_b = w_out[0].astype(BF16)
    w_up_b = w_up[0].astype(BF16)
    w_down_b = w_down[0].astype(BF16)
    wglu_t = w_glu[0].T.astype(BF16)
    bglu_col = b_glu[0].reshape(D_SSM, 1)
    gn_ssm_col = gn_ssm[0].reshape(D_SSM, 1)

    def trunk(x):
        assert x.shape[1] % TILE == 0 and x.shape[2] == D_MODEL
        nconv, zut = _mixer_in(x, pre_mix_g, w_in_b, conv_w[0], gn_conv)
        nssm = _s5_mixer(zut, wv, wy, coef, wglu_t, bglu_col, gn_ssm_col)
        return _out_ffn(x, nconv, nssm, w_out_b, post_mix_g, pre_ffn_g, w_up_b,
                        ffn_conv_w[0], ffn_conv_b, w_down_b, post_ffn_g)

    return (trunk(x_prompt), trunk(x_sample))
```

```python
import functools

import jax
import jax.numpy as jnp
import numpy as np
from jax import lax
from jax.experimental import pallas as pl
from jax.experimental.pallas import tpu as pltpu

D_MODEL = 1024
D_CONV = 512
D_SSM = 512
SSM_GROUP = 16
SSM_GROUPS = D_SSM // SSM_GROUP
SSM_STATE = 64
D_FF = 2816
EPS = 1e-6
LANES = 128

CHUNK = 16
TILE_CHUNKS = 128
TILE = CHUNK * TILE_CHUNKS
PROJ_ROWS = 512
HALO = 16
FFN_CHUNKS = 32
FFN_TILE = CHUNK * FFN_CHUNKS
MXU_COLS = 512
FF_BLOCKS = tuple((c0, min(MXU_COLS, D_FF - c0)) for c0 in range(0, D_FF, MXU_COLS))
FF_BLOCK_MAX = MXU_COLS
FF_SPLIT = FF_BLOCKS[-1][0]
PIPE_LAG = 2
SCAN_PAD = 8
PAIR_UNROLL = 4
SCAN_UNROLL = 4
GLU_COLS = 512
VMEM_LIMIT_V7X = 56 * 1024 * 1024

F32 = jnp.float32
BF16 = jnp.bfloat16


def _rms(x, g):
    return x * lax.rsqrt(jnp.mean(x * x, axis=-1, keepdims=True) + EPS) * g


def _dot(a, b):
    return jnp.dot(a, b, preferred_element_type=F32)


def _gelu_tanh(x):
    c = 0.7978845608028654
    half_x = 0.5 * x
    return half_x + half_x * jnp.tanh(x * (c + (c * 0.044715) * (x * x)))


def _interleave(heavy, light):
    i = j = 0
    while i < len(heavy) or j < len(light):
        if j >= len(light) or (i < len(heavy) and i * len(light) <= j * len(heavy)):
            heavy[i]()
            i += 1
        else:
            light[j]()
            j += 1


def _const_spec(shape):
    zeros = (0,) * len(shape)
    return pl.BlockSpec(shape, lambda *_: zeros, pipeline_mode=pl.Buffered(1))


def _s5_tables(lam_re, lam_im, log_step, b_re, b_im, c_re, c_im, d_skip):
    T, G, P, H = CHUNK, SSM_GROUPS, SSM_STATE, SSM_GROUP
    lam = lax.complex(lam_re.astype(F32), lam_im.astype(F32))
    dt = jnp.exp(log_step.astype(F32))[..., None]
    lam_bar = jnp.exp(lam * dt)
    b_bar = ((lam_bar - 1.0) / lam)[..., None] * lax.complex(b_re.astype(F32), b_im.astype(F32))
    cmat = lax.complex(c_re.astype(F32), c_im.astype(F32))
    k = jnp.arange(T + 1, dtype=F32)
    pw = jnp.exp((lam * dt)[..., None] * k)

    exact = lax.Precision.HIGHEST
    cp = cmat[:, :, None] * jnp.swapaxes(pw[..., :T], 2, 3)[:, :, :, None, :]
    kern = (jnp.einsum('dgkap,dgph->dgkah', jnp.real(cp), jnp.real(b_bar), precision=exact)
            - jnp.einsum('dgkap,dgph->dgkah', jnp.imag(cp), jnp.imag(b_bar), precision=exact))
    kern_f = kern[0].at[:, 0].add(jnp.eye(H, dtype=F32) * d_skip.astype(F32).reshape(G, 1, H))
    lag = np.arange(T)[:, None] - np.arange(T)[None, :]
    place_f = (lag[None] == np.arange(T)[:, None, None]).astype(np.float32)
    place_b = (-lag[None] == np.arange(T)[:, None, None]).astype(np.float32)
    toe = (jnp.einsum('kab,gkch->gacbh', place_f, kern_f, precision=exact)
           + jnp.einsum('kab,gkch->gacbh', place_b, kern[1], precision=exact)
           ).reshape(G, T * H, T * H)

    inc_f = pw[0][:, :, T - 1::-1][..., None] * b_bar[0][:, :, None, :]
    inc_b = pw[1][:, :, :T][..., None] * b_bar[1][:, :, None, :]
    wv = jnp.concatenate([jnp.real(inc_f), jnp.imag(inc_f), jnp.real(inc_b), jnp.imag(inc_b)],
                         axis=1).reshape(G, 4 * P, T * H)

    st_f = cmat[0][:, None] * jnp.swapaxes(pw[0][:, :, 1:], 1, 2)[:, :, None, :]
    st_b = cmat[1][:, None] * jnp.swapaxes(pw[1][:, :, T:0:-1], 1, 2)[:, :, None, :]
    mst = jnp.concatenate([jnp.real(st_f), -jnp.imag(st_f), jnp.real(st_b), -jnp.imag(st_b)],
                          axis=-1).reshape(G, T * H, 4 * P)

    a16 = pw[..., T].reshape(2, G // 2, 2 * P)
    coef = jnp.stack([jnp.real(a16[0]), jnp.imag(a16[0]), jnp.real(a16[1]), jnp.imag(a16[1])])
    return wv.astype(BF16), toe.astype(BF16), mst.astype(BF16), coef.astype(F32)


def _slab_copies(hbm4, b, c0, n_chunks, slabs, sems, to_hbm):
    copies = []
    for t in range(CHUNK):
        window = hbm4.at[b, pl.ds(c0, n_chunks), t]
        src, dst = (slabs.at[t], window) if to_hbm else (window, slabs.at[t])
        copies.append(pltpu.make_async_copy(src, dst, sems.at[t]))
    return copies


def _mixer_in_kernel(x4_ref, xp_ref, xn_ref, g_ref, win_ref, cw_ref, gn_ref,
                     nconv_ref, zut_ref, hn_even, hn_odd, xs_even, xs_odd, p_scr, zb_scr,
                     x_sems, *, nt):
    s = pl.program_id(0)
    nsteps = pl.num_programs(0)
    ntiles = nsteps - 1

    def x_copies(tile, slabs, sems):
        return _slab_copies(x4_ref, tile // nt, (tile % nt) * TILE_CHUNKS, TILE_CHUNKS,
                            slabs, sems, to_hbm=False)

    @pl.when(s == 0)
    def _():
        for ref in (hn_even, hn_odd):
            ref[...] = jnp.zeros(ref.shape, ref.dtype)
        for copy in x_copies(0, xs_even, x_sems.at[0]):
            copy.start()

    hns, xss = (hn_even, hn_odd), (xs_even, xs_odd)
    for parity in range(2):
        other = 1 - parity

        @pl.when(s % 2 == parity)
        def _(parity=parity, other=other):
            tile = jnp.minimum(s, ntiles - 1)
            for copy in x_copies(tile, xss[parity], x_sems.at[parity]):
                copy.wait()
            for copy in x_copies(jnp.minimum(s + 1, ntiles - 1), xss[other], x_sems.at[other]):
                copy.start()
            _mixer_in_step(xss[parity], xp_ref, xn_ref, g_ref, win_ref, cw_ref, gn_ref,
                           nconv_ref, zut_ref, hns[parity], hns[other], p_scr, zb_scr,
                           tile % nt, nt)

    @pl.when(s == nsteps - 1)
    def _():
        nxt = 1 - (nsteps - 1) % 2
        for copy in x_copies(ntiles - 1, xss[nxt], x_sems.at[nxt]):
            copy.wait()


def _mixer_in_step(x_slabs, xp_ref, xn_ref, g_ref, win_ref, cw_ref, gn_ref, nconv_ref, zut_ref,
                   hn_new, hn_old, p_scr, zb_scr, i, nt):
    T, NC = CHUNK, TILE_CHUNKS
    half = D_CONV // 2

    def norm_piece(t):
        if t < T:
            hn_new[t * NC:(t + 1) * NC, :] = _rms(x_slabs[t], g_ref[...]).astype(BF16)
        else:
            xh = jnp.concatenate([xp_ref[0], xn_ref[0]], axis=0)
            row = lax.broadcasted_iota(jnp.int32, (HALO, 1), 0)
            inside = ((row < 8) & (i > 0)) | ((row >= 8) & (i < nt - 1))
            hn_new[TILE:TILE + HALO, :] = jnp.where(inside, _rms(xh, g_ref[...]), 0.0).astype(BF16)

    norm_stage = [functools.partial(norm_piece, t) for t in range(T + 1)]

    nblk = TILE // PROJ_ROWS

    def block_rows(r, with_halo):
        end = TILE + HALO if (with_halo and r == nblk - 1) else (r + 1) * PROJ_ROWS
        return slice(r * PROJ_ROWS, end)

    def gate_piece(r, k):
        rows = block_rows(r, True)
        zc = _dot(hn_old[rows, :], win_ref[:, D_CONV + k * half:D_CONV + (k + 1) * half])
        zx = _dot(hn_old[rows, :], win_ref[:, 2 * D_CONV + k * half:2 * D_CONV + (k + 1) * half])
        p_scr[rows, k * half:(k + 1) * half] = zc * zx

    def zb_piece(r):
        rows = block_rows(r, False)
        zb_scr[rows, :] = _dot(hn_old[rows, :], win_ref[:, 0:D_CONV])

    def zu_piece(r):
        rows = block_rows(r, False)
        zu = _dot(hn_new[rows, :], win_ref[:, 3 * D_CONV:])
        zut_ref[0, 0, :, rows] = zu.T.astype(BF16)

    rid = lax.broadcasted_iota(jnp.int32, (NC, 1), 0)

    def conv_piece(t):
        cur = p_scr[t * NC:(t + 1) * NC, :]
        if t > 0:
            prev = p_scr[(t - 1) * NC:t * NC, :]
        else:
            prev = jnp.where(rid == 0, p_scr[TILE + 7:TILE + 8, :],
                             pltpu.roll(p_scr[(T - 1) * NC:T * NC, :], 1, 0))
        if t < T - 1:
            nxt = p_scr[(t + 1) * NC:(t + 2) * NC, :]
        else:
            nxt = jnp.where(rid == NC - 1, p_scr[TILE + 8:TILE + 9, :],
                            pltpu.roll(p_scr[0:NC, :], NC - 1, 0))
        yc = zb_scr[t * NC:(t + 1) * NC, :] * (cw_ref[0:1, :] * prev + cw_ref[1:2, :] * cur
                                               + cw_ref[2:3, :] * nxt)
        nconv_ref[0, 0, t * NC:(t + 1) * NC, :] = _rms(yc, gn_ref[...]).astype(BF16)

    gate_stage = [functools.partial(gate_piece, r, k) for r in range(nblk) for k in range(2)]
    zb_stage = [functools.partial(zb_piece, r) for r in range(nblk)]
    zu_stage = [functools.partial(zu_piece, r) for r in range(nblk)]
    conv_stage = [functools.partial(conv_piece, t) for t in range(T)]
    _interleave(gate_stage + zb_stage, norm_stage)
    _interleave(zu_stage, conv_stage)


def _mixer_in(x, pre_mix_g, w_in, conv_w, gn_conv):
    B, L, D = x.shape
    nt = L // TILE
    blk8 = TILE // 8
    ntiles = B * nt

    def in_tile(s):
        tile = jnp.minimum(s, ntiles - 1)
        return tile // nt, tile % nt

    def out_map(s):
        tile = jnp.maximum(s - 1, 0)
        return (tile // nt, tile % nt, 0, 0)

    def x_prev_map(s):
        b, i = in_tile(s)
        return (b, jnp.maximum(i * blk8 - 1, 0), 0)

    def x_next_map(s):
        b, i = in_tile(s)
        return (b, jnp.minimum((i + 1) * blk8, L // 8 - 1), 0)

    x4 = x.reshape(B, L // CHUNK, CHUNK, D)
    slab_shape = (CHUNK, TILE_CHUNKS, D)
    return pl.pallas_call(
        functools.partial(_mixer_in_kernel, nt=nt),
        grid=(ntiles + 1,),
        in_specs=[
            pl.BlockSpec(memory_space=pl.ANY),
            pl.BlockSpec((1, 8, D), x_prev_map),
            pl.BlockSpec((1, 8, D), x_next_map),
            _const_spec((1, D)),
            _const_spec((D, 4 * D_CONV)),
            _const_spec((3, D_CONV)),
            _const_spec((1, D_CONV)),
        ],
        out_specs=[
            pl.BlockSpec((1, 1, TILE, D_CONV), out_map),
            pl.BlockSpec((1, 1, D_SSM, TILE), lambda s: in_tile(s) + (0, 0)),
        ],
        out_shape=[
            jax.ShapeDtypeStruct((B, nt, TILE, D_CONV), BF16),
            jax.ShapeDtypeStruct((B, nt, D_SSM, TILE), BF16),
        ],
        scratch_shapes=[pltpu.VMEM((TILE + HALO, D), BF16),
                        pltpu.VMEM((TILE + HALO, D), BF16),
                        pltpu.VMEM(slab_shape, F32),
                        pltpu.VMEM(slab_shape, F32),
                        pltpu.VMEM((TILE + HALO, D_CONV), F32),
                        pltpu.VMEM((TILE, D_CONV), F32),
                        pltpu.SemaphoreType.DMA((2, CHUNK))],
        compiler_params=pltpu.CompilerParams(
            dimension_semantics=("arbitrary",),
            vmem_limit_bytes=VMEM_LIMIT_V7X),
        name="mixer_in",
    )(x4, x, x, pre_mix_g, w_in, conv_w, gn_conv)


def _s5_kernel(zut_ref, wv_ref, toe_ref, mst_ref, coef_ref, wglu_ref, bglu_ref, gn_ref,
               out_ref, vfr, vfi, vbr, vbi, yt_scr, *, nt):
    T, NC, H, P = CHUNK, TILE_CHUNKS, SSM_GROUP, SSM_STATE
    npair = SSM_GROUPS // 2
    nc = nt * NC
    pitch = nc + SCAN_PAD

    def chunk_inputs(grp):
        r0 = pl.multiple_of(grp * H, H)
        return jnp.concatenate(
            [jnp.concatenate([zut_ref[0, i, pl.ds(r0, H), t * NC:(t + 1) * NC] for t in range(T)],
                             axis=0) for i in range(nt)], axis=1)

    def inc_body(q, carry):
        va = _dot(wv_ref[2 * q], chunk_inputs(2 * q))
        vb = _dot(wv_ref[2 * q + 1], chunk_inputs(2 * q + 1))
        pieces = []
        for kind in range(4):
            pieces += [va[kind * P:(kind + 1) * P], vb[kind * P:(kind + 1) * P]]
        v = jnp.concatenate(pieces, axis=0).T
        r0 = pl.multiple_of(q * pitch, 8)
        vfr[pl.ds(r0, nc), :] = v[:, 0:2 * P]
        vfi[pl.ds(r0, nc), :] = v[:, 2 * P:4 * P]
        vbr[pl.ds(r0, nc), :] = v[:, 4 * P:6 * P]
        vbi[pl.ds(r0, nc), :] = v[:, 6 * P:8 * P]
        return carry
    lax.fori_loop(0, npair, inc_body, 0, unroll=PAIR_UNROLL)

    afr, afi, abr, abi = coef_ref[0], coef_ref[1], coef_ref[2], coef_ref[3]

    def scan_body(k, carry):
        sfr, sfi, sbr, sbi = carry
        rows_f = pl.ds(k, npair, stride=pitch)
        rows_b = pl.ds(nc - 1 - k, npair, stride=pitch)
        ur, ui = vfr[rows_f, :], vfi[rows_f, :]
        wr, wi = vbr[rows_b, :], vbi[rows_b, :]
        vfr[rows_f, :] = sfr
        vfi[rows_f, :] = sfi
        vbr[rows_b, :] = sbr
        vbi[rows_b, :] = sbi
        return (afr * sfr - afi * sfi + ur, afr * sfi + afi * sfr + ui,
                abr * sbr - abi * sbi + wr, abr * sbi + abi * sbr + wi)

    zero = jnp.zeros((npair, 2 * P), F32)
    lax.fori_loop(0, nc, scan_body, (zero, zero, zero, zero), unroll=SCAN_UNROLL)

    def out_body(q, carry):
        rows = pl.ds(pl.multiple_of(q * pitch, 8), nc)
        sin = jnp.concatenate([vfr[rows, :], vfi[rows, :], vbr[rows, :], vbi[rows, :]],
                              axis=1).T
        for j in range(2):
            grp = 2 * q + j
            st = jnp.concatenate([sin[(2 * kind + j) * P:(2 * kind + j + 1) * P]
                                  for kind in range(4)], axis=0)
            yt = (_dot(toe_ref[grp], chunk_inputs(grp))
                  + _dot(mst_ref[grp], st.astype(BF16)))
            h0 = pl.multiple_of(grp * H, H)
            for i in range(nt):
                for t in range(T):
                    yt_scr[i, pl.ds(h0, H), t * NC:(t + 1) * NC] = yt[t * H:(t + 1) * H,
                                                                      i * NC:(i + 1) * NC]
        return carry
    lax.fori_loop(0, npair, out_body, 0, unroll=PAIR_UNROLL)

    for i in range(nt):
        for cb in range(TILE // GLU_COLS):
            cols = slice(cb * GLU_COLS, (cb + 1) * GLU_COLS)
            y = _gelu_tanh(yt_scr[i, :, cols])
            gate = _dot(wglu_ref[...], y.astype(BF16)) + bglu_ref[...]
            y = y * jax.nn.sigmoid(gate)
            ms = jnp.mean(y * y, axis=0, keepdims=True)
            y = y * lax.rsqrt(ms + EPS) * gn_ref[...]
            out_ref[0, i, cols, :] = y.T.astype(BF16)


def _s5_mixer(zut, wv, toe, mst, coef, wglu_t, bglu_col, gn_col):
    B, nt = zut.shape[0], zut.shape[1]
    npair = SSM_GROUPS // 2
    scan_rows = npair * (nt * TILE_CHUNKS + SCAN_PAD)
    return pl.pallas_call(
        functools.partial(_s5_kernel, nt=nt),
        grid=(B,),
        in_specs=[
            pl.BlockSpec((1, nt, D_SSM, TILE), lambda b: (b, 0, 0, 0)),
            _const_spec(wv.shape),
            _const_spec(toe.shape),
            _const_spec(mst.shape),
            _const_spec(coef.shape),
            _const_spec((D_SSM, D_SSM)),
            _const_spec((D_SSM, 1)),
            _const_spec((D_SSM, 1)),
        ],
        out_specs=pl.BlockSpec((1, nt, TILE, D_SSM), lambda b: (b, 0, 0, 0)),
        out_shape=jax.ShapeDtypeStruct((B, nt, TILE, D_SSM), BF16),
        scratch_shapes=[pltpu.VMEM((scan_rows, 2 * SSM_STATE), F32) for _ in range(4)]
        + [pltpu.VMEM((nt, D_SSM, TILE), F32)],
        compiler_params=pltpu.CompilerParams(
            dimension_semantics=("parallel",),
            vmem_limit_bytes=VMEM_LIMIT_V7X),
        name="s5_mixer",
    )(zut, wv, toe, mst, coef, wglu_t, bglu_col, gn_col)


def _ffn_slab_copies(hbm4, tile, nj, slabs, sems, to_hbm):
    return _slab_copies(hbm4, tile // nj, (tile % nj) * FFN_CHUNKS, FFN_CHUNKS, slabs, sems, to_hbm)


def _out_ffn_kernel(x4_ref, xp_ref, xn_ref, nc_ref, ncp_ref, ncn_ref, ns_ref, nsp_ref, nsn_ref,
                    wout_ref, pmg_ref, pfg_ref, wup_ref, fcw_ref, fcb_ref, wdown_ref, pog_ref,
                    o4_ref, up_a, up_b, act_scr, x1_even, x1_odd, h2_even, h2_odd, f_scr,
                    mix_scr, xs_even, xs_odd, os_even, os_odd, x_sems, o_sems, *, nj):
    s = pl.program_id(0)
    nsteps = pl.num_programs(0)
    ntiles = nsteps - PIPE_LAG

    @pl.when(s == 0)
    def _():
        for ref in (x1_even, x1_odd, h2_even, h2_odd, f_scr):
            ref[...] = jnp.zeros(ref.shape, ref.dtype)
        for copy in _ffn_slab_copies(x4_ref, 0, nj, xs_even, x_sems.at[0], to_hbm=False):
            copy.start()

    shared = (xp_ref, xn_ref, nc_ref, ncp_ref, ncn_ref, ns_ref, nsp_ref, nsn_ref,
              wout_ref, pmg_ref, pfg_ref, wup_ref, fcw_ref, fcb_ref, wdown_ref, pog_ref,
              up_a, up_b, act_scr, f_scr, mix_scr)
    x1s, h2s = (x1_even, x1_odd), (h2_even, h2_odd)
    xss, oss = (xs_even, xs_odd), (os_even, os_odd)
    for parity in range(2):
        other = 1 - parity

        @pl.when(s % 2 == parity)
        def _(parity=parity, other=other):
            x_tile = jnp.minimum(s, ntiles - 1)
            for copy in _ffn_slab_copies(x4_ref, x_tile, nj, xss[parity], x_sems.at[parity], False):
                copy.wait()
            x_next = jnp.minimum(s + 1, ntiles - 1)
            for copy in _ffn_slab_copies(x4_ref, x_next, nj, xss[other], x_sems.at[other], False):
                copy.start()
            o_tile = jnp.where(s >= PIPE_LAG, s - PIPE_LAG, s)
            o_prev = jnp.where(s >= 2 * PIPE_LAG, s - 2 * PIPE_LAG, s - PIPE_LAG)

            @pl.when(s >= PIPE_LAG)
            def _():
                for copy in _ffn_slab_copies(o4_ref, o_prev, nj, oss[parity], o_sems.at[parity], True):
                    copy.wait()

            _out_ffn_step(shared, xss[parity], oss[parity], x1s[parity], h2s[parity], h2s[other],
                          s, ntiles, nj)
            for copy in _ffn_slab_copies(o4_ref, o_tile, nj, oss[parity], o_sems.at[parity], True):
                copy.start()

    @pl.when(s == nsteps - 1)
    def _():
        last_parity = (ntiles + PIPE_LAG - 1) % 2
        nxt = 1 - last_parity
        for copy in _ffn_slab_copies(x4_ref, ntiles - 1, nj, xss[nxt], x_sems.at[nxt], False):
            copy.wait()
        for par, tile in ((nxt, ntiles - 2), (last_parity, ntiles - 1)):
            for copy in _ffn_slab_copies(o4_ref, tile, nj, oss[par], o_sems.at[par], True):
                copy.wait()


def _out_ffn_step(shared, x_slabs, o_slabs, x1_tile, h2_new, h2_old, s, ntiles, nj):
    (xp_ref, xn_ref, nc_ref, ncp_ref, ncn_ref, ns_ref, nsp_ref, nsn_ref,
     wout_ref, pmg_ref, pfg_ref, wup_ref, fcw_ref, fcb_ref, wdown_ref, pog_ref,
     up_a, up_b, act_scr, f_scr, mix_scr) = shared
    j = jnp.minimum(s, ntiles - 1) % nj
    last = nj - 1
    T, NC, R = CHUNK, FFN_CHUNKS, FFN_TILE


    f_prev = f_scr[...]
    f_scale = lax.rsqrt(jnp.mean(f_prev * f_prev, axis=-1, keepdims=True) + EPS)

    def out_piece(t):
        rows = slice(t * NC, (t + 1) * NC)
        o_slabs[t] = x1_tile[rows, :] + f_scr[rows, :] * f_scale[rows] * pog_ref[...]

    out_stage = [functools.partial(out_piece, t) for t in range(T)]

    def with_halo(main_ref, prev_ref, next_ref):
        main = main_ref[0, 0].reshape(R, main_ref.shape[-1])
        halo = jnp.concatenate([prev_ref[0, 0, 0].astype(F32)[8:16],
                                next_ref[0, 0, 0].astype(F32)[0:8]], axis=0).astype(BF16)
        return jnp.concatenate([main, halo], axis=0)

    def proj_piece(c):
        cols = slice(c * MXU_COLS, (c + 1) * MXU_COLS)
        lhs = jnp.concatenate([with_halo(nc_ref, ncp_ref, ncn_ref),
                               with_halo(ns_ref, nsp_ref, nsn_ref)], axis=1)
        mix_scr[:, cols] = _dot(lhs, wout_ref[:, cols])

    def norm_piece(r):
        if r < T:
            rows = slice(r * NC, (r + 1) * NC)
            x1 = x_slabs[r] + _rms(mix_scr[rows, :], pmg_ref[...])
            x1_tile[rows, :] = x1
            h2_new[rows, :] = _rms(x1, pfg_ref[...]).astype(BF16)
        else:
            rows = slice(R, R + HALO)
            xr = jnp.concatenate([xp_ref[0], xn_ref[0]], axis=0)
            x1 = xr + _rms(mix_scr[rows, :], pmg_ref[...])
            row = lax.broadcasted_iota(jnp.int32, (HALO, 1), 0)
            inside = ((row < 8) & (j > 0)) | ((row >= 8) & (j < last))
            h2_new[rows, :] = jnp.where(inside, _rms(x1, pfg_ref[...]), 0.0).astype(BF16)

    proj_stage = [functools.partial(proj_piece, c) for c in range(D_MODEL // MXU_COLS)]
    norm_stage = [functools.partial(norm_piece, r) for r in range(T + 1)]

    rid = lax.broadcasted_iota(jnp.int32, (NC, 1), 0)

    def up_piece(k, part):
        c0, width = FF_BLOCKS[k]
        up = (up_a, up_b)[k % 2]
        res = _dot(h2_old[...], wup_ref[:, part * D_FF + c0:part * D_FF + c0 + width])
        nq = width // LANES
        for q in range(nq):
            up[part * nq + q] = res[:, q * LANES:(q + 1) * LANES]

    def conv_taps(up, blk, cols, t):
        cur = up[blk, t * NC:(t + 1) * NC, :]
        if t > 0:
            prev = up[blk, (t - 1) * NC:t * NC, :]
        else:
            prev = jnp.where(rid == 0, up[blk, R + 7:R + 8, :],
                             pltpu.roll(up[blk, (T - 1) * NC:T * NC, :], 1, 0))
        if t < T - 1:
            nxt = up[blk, (t + 1) * NC:(t + 2) * NC, :]
        else:
            nxt = jnp.where(rid == NC - 1, up[blk, R + 8:R + 9, :],
                            pltpu.roll(up[blk, 0:NC, :], NC - 1, 0))
        return (fcw_ref[0:1, cols] * prev + fcw_ref[1:2, cols] * cur + fcw_ref[2:3, cols] * nxt
                + fcb_ref[:, cols])

    def conv_piece(k, t):
        c0, width = FF_BLOCKS[k]
        up = (up_a, up_b)[k % 2]
        nq = width // LANES
        for q in range(nq):
            gcols = slice(c0 + q * LANES, c0 + (q + 1) * LANES)
            vcols = slice(D_FF + c0 + q * LANES, D_FF + c0 + (q + 1) * LANES)
            act = jax.nn.silu(conv_taps(up, q, gcols, t)) * conv_taps(up, nq + q, vcols, t)
            act_scr[t * NC:(t + 1) * NC, gcols] = act.astype(BF16)

    def down_piece(first, c):
        cols = slice(c * MXU_COLS, (c + 1) * MXU_COLS)
        if first:
            f_scr[:, cols] = _dot(act_scr[:, 0:FF_SPLIT], wdown_ref[0:FF_SPLIT, cols])
        else:
            f_scr[:, cols] += _dot(act_scr[:, FF_SPLIT:], wdown_ref[FF_SPLIT:, cols])

    nblk = len(FF_BLOCKS)
    up_stage = [[functools.partial(up_piece, k, part) for part in range(2)] for k in range(nblk)]
    conv_stage = [[functools.partial(conv_piece, k, t) for t in range(T)] for k in range(nblk)]
    down_stage = [[functools.partial(down_piece, first, c) for c in range(D_MODEL // MXU_COLS)]
                  for first in (True, False)]
    assert FF_SPLIT <= FF_BLOCKS[-1][0]

    _interleave(proj_stage, out_stage)
    _interleave(up_stage[0], norm_stage)
    for k in range(nblk):
        heavy = up_stage[k + 1] if k + 1 < nblk else down_stage[0]
        _interleave(heavy, conv_stage[k])
    _interleave(down_stage[1], [])


def _out_ffn(x, nconv, nssm, w_out, post_mix_g, pre_ffn_g, w_up, ffn_conv_w, ffn_conv_b,
             w_down, post_ffn_g):
    B, L, D = x.shape
    nt = L // TILE
    nj = L // FFN_TILE
    per_tile = TILE_CHUNKS // FFN_CHUNKS
    blk8 = FFN_TILE // 8
    nchunks = L // CHUNK
    nc5 = nconv.reshape(B, nt, CHUNK, TILE_CHUNKS, D_CONV)
    ns5 = nssm.reshape(B, nt, CHUNK, TILE_CHUNKS, D_SSM)

    ntiles = B * nj
    up_shape = (2 * FF_BLOCK_MAX // LANES, FFN_TILE + HALO, LANES)

    def in_tile(s):
        tile = jnp.minimum(s, ntiles - 1)
        return tile // nj, tile % nj

    def x_prev_map(s):
        b, j = in_tile(s)
        return (b, jnp.maximum(j * blk8 - 1, 0), 0)

    def x_next_map(s):
        b, j = in_tile(s)
        return (b, jnp.minimum((j + 1) * blk8, L // 8 - 1), 0)

    def main_map(s):
        b, j = in_tile(s)
        return (b, j // per_tile, 0, j % per_tile, 0)

    def prev_map(s):
        b, j = in_tile(s)
        c = jnp.maximum(j * FFN_CHUNKS - 1, 0)
        return (b, c // TILE_CHUNKS, CHUNK - 1, (c % TILE_CHUNKS) // 16, 0)

    def next_map(s):
        b, j = in_tile(s)
        c = jnp.minimum((j + 1) * FFN_CHUNKS, nchunks - 1)
        return (b, c // TILE_CHUNKS, 0, (c % TILE_CHUNKS) // 16, 0)

    act_specs = []
    for width in (D_CONV, D_SSM):
        act_specs += [pl.BlockSpec((1, 1, CHUNK, FFN_CHUNKS, width), main_map),
                      pl.BlockSpec((1, 1, 1, 16, width), prev_map),
                      pl.BlockSpec((1, 1, 1, 16, width), next_map)]
    x4 = x.reshape(B, nchunks, CHUNK, D)
    slab_shape = (CHUNK, FFN_CHUNKS, D)
    out4 = pl.pallas_call(
        functools.partial(_out_ffn_kernel, nj=nj),
        grid=(ntiles + PIPE_LAG,),
        in_specs=[
            pl.BlockSpec(memory_space=pl.ANY),
            pl.BlockSpec((1, 8, D), x_prev_map),
            pl.BlockSpec((1, 8, D), x_next_map),
        ] + act_specs + [
            _const_spec((D_CONV + D_SSM, D)),
            _const_spec((1, D)),
            _const_spec((1, D)),
            _const_spec((D, 2 * D_FF)),
            _const_spec((3, 2 * D_FF)),
            _const_spec((1, 2 * D_FF)),
            _const_spec((D_FF, D)),
            _const_spec((1, D)),
        ],
        out_specs=pl.BlockSpec(memory_space=pl.ANY),
        out_shape=jax.ShapeDtypeStruct((B, nchunks, CHUNK, D), F32),
        scratch_shapes=[pltpu.VMEM(up_shape, F32),
                        pltpu.VMEM(up_shape, F32),
                        pltpu.VMEM((FFN_TILE, D_FF), BF16),
                        pltpu.VMEM((FFN_TILE, D), F32),
                        pltpu.VMEM((FFN_TILE, D), F32),
                        pltpu.VMEM((FFN_TILE + HALO, D), BF16),
                        pltpu.VMEM((FFN_TILE + HALO, D), BF16),
                        pltpu.VMEM((FFN_TILE, D), F32),
                        pltpu.VMEM((FFN_TILE + HALO, D), F32),
                        pltpu.VMEM(slab_shape, F32),
                        pltpu.VMEM(slab_shape, F32),
                        pltpu.VMEM(slab_shape, F32),
                        pltpu.VMEM(slab_shape, F32),
                        pltpu.SemaphoreType.DMA((2, CHUNK)),
                        pltpu.SemaphoreType.DMA((2, CHUNK))],
        compiler_params=pltpu.CompilerParams(
            dimension_semantics=("arbitrary",),
            vmem_limit_bytes=VMEM_LIMIT_V7X),
        name="out_ffn",
    )(x4, x, x, nc5, nc5, nc5, ns5, ns5, ns5, w_out, post_mix_g, pre_ffn_g, w_up,
      ffn_conv_w, ffn_conv_b, w_down, post_ffn_g)
    return out4.reshape(B, L, D)


def kernel(x_prompt, x_sample, pre_mix_g, w_in, conv_w, lam_re, lam_im, log_step, b_re, b_im,
           c_re, c_im, d_skip, w_glu, b_glu, gn_conv, gn_ssm, w_out, post_mix_g,
           pre_ffn_g, w_up, ffn_conv_w, ffn_conv_b, w_down, post_ffn_g):
    assert pre_mix_g.shape[0] == 1, "one encoder layer"
    wv, toe, mst, coef = _s5_tables(lam_re[0], lam_im[0], log_step[0], b_re[0], b_im[0],
                              c_re[0], c_im[0], d_skip[0])
    w_in_b = w_in[0].astype(BF16)
    w_out_b = w_out[0].astype(BF16)
    w_up_b = w_up[0].astype(BF16)
    w_down_b = w_down[0].astype(BF16)
    wglu_t = w_glu[0].T.astype(BF16)
    bglu_col = b_glu[0].reshape(D_SSM, 1)
    gn_ssm_col = gn_ssm[0].reshape(D_SSM, 1)

    def trunk(x):
        assert x.shape[1] % TILE == 0 and x.shape[2] == D_MODEL
        nconv, zut = _mixer_in(x, pre_mix_g, w_in_b, conv_w[0], gn_conv)
        nssm = _s5_mixer(zut, wv, toe, mst, coef, wglu_t, bglu_col, gn_ssm_col)
        return _out_ffn(x, nconv, nssm, w_out_b, post_mix_g, pre_ffn_g, w_up_b,
                        ffn_conv_w[0], ffn_conv_b, w_down_b, post_ffn_g)

    return (trunk(x_prompt), trunk(x_sample))
```

```python
import functools

import jax
import jax.numpy as jnp
from jax import lax
from jax.experimental import pallas as pl
from jax.experimental.pallas import tpu as pltpu

D_MODEL = 1024
D_CONV = 512
D_SSM = 512
SSM_GROUP = 16
SSM_GROUPS = D_SSM // SSM_GROUP
SSM_STATE = 64
D_FF = 2816
EPS = 1e-6
LANES = 128

CHUNK = 16
TILE_CHUNKS = 128
TILE = CHUNK * TILE_CHUNKS
PROJ_ROWS = 512
HALO = 16
FFN_CHUNKS = 32
FFN_TILE = CHUNK * FFN_CHUNKS
MXU_COLS = 512
FF_BLOCKS = tuple((c0, min(MXU_COLS, D_FF - c0)) for c0 in range(0, D_FF, MXU_COLS))
FF_BLOCK_MAX = MXU_COLS
FF_SPLIT = FF_BLOCKS[-1][0]
PIPE_LAG = 2
SCAN_PAD = 8
PAIR_UNROLL = 4
SCAN_UNROLL = 4
GLU_COLS = 512
VMEM_LIMIT_V7X = 56 * 1024 * 1024

F32 = jnp.float32
BF16 = jnp.bfloat16


def _rms(x, g):
    return x * lax.rsqrt(jnp.mean(x * x, axis=-1, keepdims=True) + EPS) * g


def _dot(a, b):
    return jnp.dot(a, b, preferred_element_type=F32)


def _gelu_tanh(x):
    c = 0.7978845608028654
    half_x = 0.5 * x
    return half_x + half_x * jnp.tanh(x * (c + (c * 0.044715) * (x * x)))


def _interleave(heavy, light):
    i = j = 0
    while i < len(heavy) or j < len(light):
        if j >= len(light) or (i < len(heavy) and i * len(light) <= j * len(heavy)):
            heavy[i]()
            i += 1
        else:
            light[j]()
            j += 1


def _const_spec(shape):
    zeros = (0,) * len(shape)
    return pl.BlockSpec(shape, lambda *_: zeros, pipeline_mode=pl.Buffered(1))


def _s5_tables(lam_re, lam_im, log_step, b_re, b_im, c_re, c_im, d_skip):
    T, G, P, H = CHUNK, SSM_GROUPS, SSM_STATE, SSM_GROUP
    lam = lax.complex(lam_re.astype(F32), lam_im.astype(F32))
    dt = jnp.exp(log_step.astype(F32))[..., None]
    lam_bar = jnp.exp(lam * dt)
    b_bar = ((lam_bar - 1.0) / lam)[..., None] * lax.complex(b_re.astype(F32), b_im.astype(F32))
    cmat = lax.complex(c_re.astype(F32), c_im.astype(F32))
    k = jnp.arange(T + 1, dtype=F32)
    pw = jnp.exp((lam * dt)[..., None] * k)

    exact = lax.Precision.HIGHEST
    cp = cmat[:, :, None] * jnp.swapaxes(pw[..., :T], 2, 3)[:, :, :, None, :]
    kern = (jnp.einsum('dgkap,dgph->dgkah', jnp.real(cp), jnp.real(b_bar), precision=exact)
            - jnp.einsum('dgkap,dgph->dgkah', jnp.imag(cp), jnp.imag(b_bar), precision=exact))
    lag0 = kern[0][:, 0] + kern[1][:, 0] + jnp.eye(H, dtype=F32) * d_skip.astype(F32).reshape(G, 1, H)
    lags = jnp.concatenate([kern[0][:, :0:-1], lag0[:, None], kern[1][:, 1:],
                            jnp.zeros((G, 1, H, H), F32)], axis=1)
    zker = lags.transpose(0, 2, 1, 3).reshape(G, H, 2 * T * H)

    inc_f = pw[0][:, :, T - 1::-1][..., None] * b_bar[0][:, :, None, :]
    inc_b = pw[1][:, :, :T][..., None] * b_bar[1][:, :, None, :]
    wv = jnp.concatenate([jnp.real(inc_f), jnp.imag(inc_f), jnp.real(inc_b), jnp.imag(inc_b)],
                         axis=1).reshape(G, 4 * P, T * H)

    st_f = cmat[0][:, None] * jnp.swapaxes(pw[0][:, :, 1:], 1, 2)[:, :, None, :]
    st_b = cmat[1][:, None] * jnp.swapaxes(pw[1][:, :, T:0:-1], 1, 2)[:, :, None, :]
    mst = jnp.concatenate([jnp.real(st_f), -jnp.imag(st_f), jnp.real(st_b), -jnp.imag(st_b)],
                          axis=-1).reshape(G, T * H, 4 * P)

    a16 = pw[..., T].reshape(2, G // 2, 2 * P)
    coef = jnp.stack([jnp.real(a16[0]), jnp.imag(a16[0]), jnp.real(a16[1]), jnp.imag(a16[1])])
    return wv.astype(BF16), zker, mst.astype(BF16), coef.astype(F32)


def _slab_copies(hbm4, b, c0, n_chunks, slabs, sems, to_hbm):
    copies = []
    for t in range(CHUNK):
        window = hbm4.at[b, pl.ds(c0, n_chunks), t]
        src, dst = (slabs.at[t], window) if to_hbm else (window, slabs.at[t])
        copies.append(pltpu.make_async_copy(src, dst, sems.at[t]))
    return copies


def _mixer_in_kernel(x4_ref, xp_ref, xn_ref, g_ref, win_ref, cw_ref, gn_ref,
                     nconv_ref, zut_ref, hn_even, hn_odd, xs_even, xs_odd, p_scr, zb_scr,
                     x_sems, *, nt):
    s = pl.program_id(0)
    nsteps = pl.num_programs(0)
    ntiles = nsteps - 1

    def x_copies(tile, slabs, sems):
        return _slab_copies(x4_ref, tile // nt, (tile % nt) * TILE_CHUNKS, TILE_CHUNKS,
                            slabs, sems, to_hbm=False)

    @pl.when(s == 0)
    def _():
        for ref in (hn_even, hn_odd):
            ref[...] = jnp.zeros(ref.shape, ref.dtype)
        for copy in x_copies(0, xs_even, x_sems.at[0]):
            copy.start()

    hns, xss = (hn_even, hn_odd), (xs_even, xs_odd)
    for parity in range(2):
        other = 1 - parity

        @pl.when(s % 2 == parity)
        def _(parity=parity, other=other):
            tile = jnp.minimum(s, ntiles - 1)
            for copy in x_copies(tile, xss[parity], x_sems.at[parity]):
                copy.wait()
            for copy in x_copies(jnp.minimum(s + 1, ntiles - 1), xss[other], x_sems.at[other]):
                copy.start()
            _mixer_in_step(xss[parity], xp_ref, xn_ref, g_ref, win_ref, cw_ref, gn_ref,
                           nconv_ref, zut_ref, hns[parity], hns[other], p_scr, zb_scr,
                           tile % nt, nt)

    @pl.when(s == nsteps - 1)
    def _():
        nxt = 1 - (nsteps - 1) % 2
        for copy in x_copies(ntiles - 1, xss[nxt], x_sems.at[nxt]):
            copy.wait()


def _mixer_in_step(x_slabs, xp_ref, xn_ref, g_ref, win_ref, cw_ref, gn_ref, nconv_ref, zut_ref,
                   hn_new, hn_old, p_scr, zb_scr, i, nt):
    T, NC = CHUNK, TILE_CHUNKS
    half = D_CONV // 2

    def norm_piece(t):
        if t < T:
            hn_new[t * NC:(t + 1) * NC, :] = _rms(x_slabs[t], g_ref[...]).astype(BF16)
        else:
            xh = jnp.concatenate([xp_ref[0], xn_ref[0]], axis=0)
            row = lax.broadcasted_iota(jnp.int32, (HALO, 1), 0)
            inside = ((row < 8) & (i > 0)) | ((row >= 8) & (i < nt - 1))
            hn_new[TILE:TILE + HALO, :] = jnp.where(inside, _rms(xh, g_ref[...]), 0.0).astype(BF16)

    norm_stage = [functools.partial(norm_piece, t) for t in range(T + 1)]

    nblk = TILE // PROJ_ROWS

    def block_rows(r, with_halo):
        end = TILE + HALO if (with_halo and r == nblk - 1) else (r + 1) * PROJ_ROWS
        return slice(r * PROJ_ROWS, end)

    def gate_piece(r, k):
        rows = block_rows(r, True)
        zc = _dot(hn_old[rows, :], win_ref[:, D_CONV + k * half:D_CONV + (k + 1) * half])
        zx = _dot(hn_old[rows, :], win_ref[:, 2 * D_CONV + k * half:2 * D_CONV + (k + 1) * half])
        p_scr[rows, k * half:(k + 1) * half] = zc * zx

    def zb_piece(r):
        rows = block_rows(r, False)
        zb_scr[rows, :] = _dot(hn_old[rows, :], win_ref[:, 0:D_CONV])

    def zu_piece(r):
        rows = block_rows(r, False)
        zu = _dot(hn_new[rows, :], win_ref[:, 3 * D_CONV:])
        zut_ref[0, 0, :, rows] = zu.T.astype(BF16)

    rid = lax.broadcasted_iota(jnp.int32, (NC, 1), 0)

    def conv_piece(t):
        cur = p_scr[t * NC:(t + 1) * NC, :]
        if t > 0:
            prev = p_scr[(t - 1) * NC:t * NC, :]
        else:
            prev = jnp.where(rid == 0, p_scr[TILE + 7:TILE + 8, :],
                             pltpu.roll(p_scr[(T - 1) * NC:T * NC, :], 1, 0))
        if t < T - 1:
            nxt = p_scr[(t + 1) * NC:(t + 2) * NC, :]
        else:
            nxt = jnp.where(rid == NC - 1, p_scr[TILE + 8:TILE + 9, :],
                            pltpu.roll(p_scr[0:NC, :], NC - 1, 0))
        yc = zb_scr[t * NC:(t + 1) * NC, :] * (cw_ref[0:1, :] * prev + cw_ref[1:2, :] * cur
                                               + cw_ref[2:3, :] * nxt)
        nconv_ref[0, 0, t * NC:(t + 1) * NC, :] = _rms(yc, gn_ref[...]).astype(BF16)

    gate_stage = [functools.partial(gate_piece, r, k) for r in range(nblk) for k in range(2)]
    zb_stage = [functools.partial(zb_piece, r) for r in range(nblk)]
    zu_stage = [functools.partial(zu_piece, r) for r in range(nblk)]
    conv_stage = [functools.partial(conv_piece, t) for t in range(T)]
    _interleave(gate_stage + zb_stage, norm_stage)
    _interleave(zu_stage, conv_stage)


def _mixer_in(x, pre_mix_g, w_in, conv_w, gn_conv):
    B, L, D = x.shape
    nt = L // TILE
    blk8 = TILE // 8
    ntiles = B * nt

    def in_tile(s):
        tile = jnp.minimum(s, ntiles - 1)
        return tile // nt, tile % nt

    def out_map(s):
        tile = jnp.maximum(s - 1, 0)
        return (tile // nt, tile % nt, 0, 0)

    def x_prev_map(s):
        b, i = in_tile(s)
        return (b, jnp.maximum(i * blk8 - 1, 0), 0)

    def x_next_map(s):
        b, i = in_tile(s)
        return (b, jnp.minimum((i + 1) * blk8, L // 8 - 1), 0)

    x4 = x.reshape(B, L // CHUNK, CHUNK, D)
    slab_shape = (CHUNK, TILE_CHUNKS, D)
    return pl.pallas_call(
        functools.partial(_mixer_in_kernel, nt=nt),
        grid=(ntiles + 1,),
        in_specs=[
            pl.BlockSpec(memory_space=pl.ANY),
            pl.BlockSpec((1, 8, D), x_prev_map),
            pl.BlockSpec((1, 8, D), x_next_map),
            _const_spec((1, D)),
            _const_spec((D, 4 * D_CONV)),
            _const_spec((3, D_CONV)),
            _const_spec((1, D_CONV)),
        ],
        out_specs=[
            pl.BlockSpec((1, 1, TILE, D_CONV), out_map),
            pl.BlockSpec((1, 1, D_SSM, TILE), lambda s: in_tile(s) + (0, 0)),
        ],
        out_shape=[
            jax.ShapeDtypeStruct((B, nt, TILE, D_CONV), BF16),
            jax.ShapeDtypeStruct((B, nt, D_SSM, TILE), BF16),
        ],
        scratch_shapes=[pltpu.VMEM((TILE + HALO, D), BF16),
                        pltpu.VMEM((TILE + HALO, D), BF16),
                        pltpu.VMEM(slab_shape, F32),
                        pltpu.VMEM(slab_shape, F32),
                        pltpu.VMEM((TILE + HALO, D_CONV), F32),
                        pltpu.VMEM((TILE, D_CONV), F32),
                        pltpu.SemaphoreType.DMA((2, CHUNK))],
        compiler_params=pltpu.CompilerParams(
            dimension_semantics=("arbitrary",),
            vmem_limit_bytes=VMEM_LIMIT_V7X),
        name="mixer_in",
    )(x4, x, x, pre_mix_g, w_in, conv_w, gn_conv)


def _s5_kernel(zut_ref, wv_ref, zker_ref, mst_ref, coef_ref, wglu_ref, bglu_ref, gn_ref,
               out_ref, vfr, vfi, vbr, vbi, yt_scr, toe_ref, zbuf, *, nt):
    T, NC, H, P = CHUNK, TILE_CHUNKS, SSM_GROUP, SSM_STATE
    npair = SSM_GROUPS // 2
    nc = nt * NC
    pitch = nc + SCAN_PAD

    @pl.when(pl.program_id(0) == 0)
    def _():
        width = 2 * T * H

        def expand(g, carry):
            pltpu.sync_copy(zker_ref.at[g], zbuf)
            z = zbuf[...]
            for t in range(T):
                off = (T - 1 - t) * H
                win = pltpu.roll(z, (width - off) % width, 1) if off else z
                toe_ref[g, t * H:(t + 1) * H, :] = win[:, 0:T * H].astype(BF16)
            return carry
        lax.fori_loop(0, SSM_GROUPS, expand, 0)

    def chunk_inputs(grp):
        r0 = pl.multiple_of(grp * H, H)
        return jnp.concatenate(
            [jnp.concatenate([zut_ref[0, i, pl.ds(r0, H), t * NC:(t + 1) * NC] for t in range(T)],
                             axis=0) for i in range(nt)], axis=1)

    def inc_body(q, carry):
        va = _dot(wv_ref[2 * q], chunk_inputs(2 * q))
        vb = _dot(wv_ref[2 * q + 1], chunk_inputs(2 * q + 1))
        pieces = []
        for kind in range(4):
            pieces += [va[kind * P:(kind + 1) * P], vb[kind * P:(kind + 1) * P]]
        v = jnp.concatenate(pieces, axis=0).T
        r0 = pl.multiple_of(q * pitch, 8)
        vfr[pl.ds(r0, nc), :] = v[:, 0:2 * P]
        vfi[pl.ds(r0, nc), :] = v[:, 2 * P:4 * P]
        vbr[pl.ds(r0, nc), :] = v[:, 4 * P:6 * P]
        vbi[pl.ds(r0, nc), :] = v[:, 6 * P:8 * P]
        return carry
    lax.fori_loop(0, npair, inc_body, 0, unroll=PAIR_UNROLL)

    afr, afi, abr, abi = coef_ref[0], coef_ref[1], coef_ref[2], coef_ref[3]

    def scan_body(k, carry):
        sfr, sfi, sbr, sbi = carry
        rows_f = pl.ds(k, npair, stride=pitch)
        rows_b = pl.ds(nc - 1 - k, npair, stride=pitch)
        ur, ui = vfr[rows_f, :], vfi[rows_f, :]
        wr, wi = vbr[rows_b, :], vbi[rows_b, :]
        vfr[rows_f, :] = sfr
        vfi[rows_f, :] = sfi
        vbr[rows_b, :] = sbr
        vbi[rows_b, :] = sbi
        return (afr * sfr - afi * sfi + ur, afr * sfi + afi * sfr + ui,
                abr * sbr - abi * sbi + wr, abr * sbi + abi * sbr + wi)

    zero = jnp.zeros((npair, 2 * P), F32)
    lax.fori_loop(0, nc, scan_body, (zero, zero, zero, zero), unroll=SCAN_UNROLL)

    def out_body(q, carry):
        rows = pl.ds(pl.multiple_of(q * pitch, 8), nc)
        sin = jnp.concatenate([vfr[rows, :], vfi[rows, :], vbr[rows, :], vbi[rows, :]],
                              axis=1).T
        for j in range(2):
            grp = 2 * q + j
            st = jnp.concatenate([sin[(2 * kind + j) * P:(2 * kind + j + 1) * P]
                                  for kind in range(4)], axis=0)
            yt = (_dot(toe_ref[grp], chunk_inputs(grp))
                  + _dot(mst_ref[grp], st.astype(BF16)))
            h0 = pl.multiple_of(grp * H, H)
            for i in range(nt):
                for t in range(T):
                    yt_scr[i, pl.ds(h0, H), t * NC:(t + 1) * NC] = yt[t * H:(t + 1) * H,
                                                                      i * NC:(i + 1) * NC]
        return carry
    lax.fori_loop(0, npair, out_body, 0, unroll=PAIR_UNROLL)

    for i in range(nt):
        for cb in range(TILE // GLU_COLS):
            cols = slice(cb * GLU_COLS, (cb + 1) * GLU_COLS)
            y = _gelu_tanh(yt_scr[i, :, cols])
            gate = _dot(wglu_ref[...], y.astype(BF16)) + bglu_ref[...]
            y = y * jax.nn.sigmoid(gate)
            ms = jnp.mean(y * y, axis=0, keepdims=True)
            y = y * lax.rsqrt(ms + EPS) * gn_ref[...]
            out_ref[0, i, cols, :] = y.T.astype(BF16)


def _s5_mixer(zut, wv, zker, mst, coef, wglu_t, bglu_col, gn_col):
    B, nt = zut.shape[0], zut.shape[1]
    npair = SSM_GROUPS // 2
    scan_rows = npair * (nt * TILE_CHUNKS + SCAN_PAD)
    return pl.pallas_call(
        functools.partial(_s5_kernel, nt=nt),
        grid=(B,),
        in_specs=[
            pl.BlockSpec((1, nt, D_SSM, TILE), lambda b: (b, 0, 0, 0)),
            _const_spec(wv.shape),
            pl.BlockSpec(memory_space=pl.ANY),
            _const_spec(mst.shape),
            _const_spec(coef.shape),
            _const_spec((D_SSM, D_SSM)),
            _const_spec((D_SSM, 1)),
            _const_spec((D_SSM, 1)),
        ],
        out_specs=pl.BlockSpec((1, nt, TILE, D_SSM), lambda b: (b, 0, 0, 0)),
        out_shape=jax.ShapeDtypeStruct((B, nt, TILE, D_SSM), BF16),
        scratch_shapes=[pltpu.VMEM((scan_rows, 2 * SSM_STATE), F32) for _ in range(4)]
        + [pltpu.VMEM((nt, D_SSM, TILE), F32),
           pltpu.VMEM((SSM_GROUPS, CHUNK * SSM_GROUP, CHUNK * SSM_GROUP), BF16),
           pltpu.VMEM((SSM_GROUP, 2 * CHUNK * SSM_GROUP), F32)],
        compiler_params=pltpu.CompilerParams(
            dimension_semantics=("arbitrary",),
            vmem_limit_bytes=VMEM_LIMIT_V7X),
        name="s5_mixer",
    )(zut, wv, zker, mst, coef, wglu_t, bglu_col, gn_col)


def _ffn_slab_copies(hbm4, tile, nj, slabs, sems, to_hbm):
    return _slab_copies(hbm4, tile // nj, (tile % nj) * FFN_CHUNKS, FFN_CHUNKS, slabs, sems, to_hbm)


def _out_ffn_kernel(x4_ref, xp_ref, xn_ref, nc_ref, ncp_ref, ncn_ref, ns_ref, nsp_ref, nsn_ref,
                    wout_ref, pmg_ref, pfg_ref, wup_ref, fcw_ref, fcb_ref, wdown_ref, pog_ref,
                    o4_ref, up_a, up_b, act_scr, x1_even, x1_odd, h2_even, h2_odd, f_scr,
                    mix_scr, xs_even, xs_odd, os_even, os_odd, x_sems, o_sems, *, nj):
    s = pl.program_id(0)
    nsteps = pl.num_programs(0)
    ntiles = nsteps - PIPE_LAG

    @pl.when(s == 0)
    def _():
        for ref in (x1_even, x1_odd, h2_even, h2_odd, f_scr):
            ref[...] = jnp.zeros(ref.shape, ref.dtype)
        for copy in _ffn_slab_copies(x4_ref, 0, nj, xs_even, x_sems.at[0], to_hbm=False):
            copy.start()

    shared = (xp_ref, xn_ref, nc_ref, ncp_ref, ncn_ref, ns_ref, nsp_ref, nsn_ref,
              wout_ref, pmg_ref, pfg_ref, wup_ref, fcw_ref, fcb_ref, wdown_ref, pog_ref,
              up_a, up_b, act_scr, f_scr, mix_scr)
    x1s, h2s = (x1_even, x1_odd), (h2_even, h2_odd)
    xss, oss = (xs_even, xs_odd), (os_even, os_odd)
    for parity in range(2):
        other = 1 - parity

        @pl.when(s % 2 == parity)
        def _(parity=parity, other=other):
            x_tile = jnp.minimum(s, ntiles - 1)
            for copy in _ffn_slab_copies(x4_ref, x_tile, nj, xss[parity], x_sems.at[parity], False):
                copy.wait()
            x_next = jnp.minimum(s + 1, ntiles - 1)
            for copy in _ffn_slab_copies(x4_ref, x_next, nj, xss[other], x_sems.at[other], False):
                copy.start()
            o_tile = jnp.where(s >= PIPE_LAG, s - PIPE_LAG, s)
            o_prev = jnp.where(s >= 2 * PIPE_LAG, s - 2 * PIPE_LAG, s - PIPE_LAG)

            @pl.when(s >= PIPE_LAG)
            def _():
                for copy in _ffn_slab_copies(o4_ref, o_prev, nj, oss[parity], o_sems.at[parity], True):
                    copy.wait()

            _out_ffn_step(shared, xss[parity], oss[parity], x1s[parity], h2s[parity], h2s[other],
                          s, ntiles, nj)
            for copy in _ffn_slab_copies(o4_ref, o_tile, nj, oss[parity], o_sems.at[parity], True):
                copy.start()

    @pl.when(s == nsteps - 1)
    def _():
        last_parity = (ntiles + PIPE_LAG - 1) % 2
        nxt = 1 - last_parity
        for copy in _ffn_slab_copies(x4_ref, ntiles - 1, nj, xss[nxt], x_sems.at[nxt], False):
            copy.wait()
        for par, tile in ((nxt, ntiles - 2), (last_parity, ntiles - 1)):
            for copy in _ffn_slab_copies(o4_ref, tile, nj, oss[par], o_sems.at[par], True):
                copy.wait()


def _out_ffn_step(shared, x_slabs, o_slabs, x1_tile, h2_new, h2_old, s, ntiles, nj):
    (xp_ref, xn_ref, nc_ref, ncp_ref, ncn_ref, ns_ref, nsp_ref, nsn_ref,
     wout_ref, pmg_ref, pfg_ref, wup_ref, fcw_ref, fcb_ref, wdown_ref, pog_ref,
     up_a, up_b, act_scr, f_scr, mix_scr) = shared
    j = jnp.minimum(s, ntiles - 1) % nj
    last = nj - 1
    T, NC, R = CHUNK, FFN_CHUNKS, FFN_TILE


    f_prev = f_scr[...]
    f_scale = lax.rsqrt(jnp.mean(f_prev * f_prev, axis=-1, keepdims=True) + EPS)

    def out_piece(t):
        rows = slice(t * NC, (t + 1) * NC)
        o_slabs[t] = x1_tile[rows, :] + f_scr[rows, :] * f_scale[rows] * pog_ref[...]

    out_stage = [functools.partial(out_piece, t) for t in range(T)]

    def with_halo(main_ref, prev_ref, next_ref):
        main = main_ref[0, 0].reshape(R, main_ref.shape[-1])
        halo = jnp.concatenate([prev_ref[0, 0, 0].astype(F32)[8:16],
                                next_ref[0, 0, 0].astype(F32)[0:8]], axis=0).astype(BF16)
        return jnp.concatenate([main, halo], axis=0)

    def proj_piece(c):
        cols = slice(c * MXU_COLS, (c + 1) * MXU_COLS)
        lhs = jnp.concatenate([with_halo(nc_ref, ncp_ref, ncn_ref),
                               with_halo(ns_ref, nsp_ref, nsn_ref)], axis=1)
        mix_scr[:, cols] = _dot(lhs, wout_ref[:, cols])

    def norm_piece(r):
        if r < T:
            rows = slice(r * NC, (r + 1) * NC)
            x1 = x_slabs[r] + _rms(mix_scr[rows, :], pmg_ref[...])
            x1_tile[rows, :] = x1
            h2_new[rows, :] = _rms(x1, pfg_ref[...]).astype(BF16)
        else:
            rows = slice(R, R + HALO)
            xr = jnp.concatenate([xp_ref[0], xn_ref[0]], axis=0)
            x1 = xr + _rms(mix_scr[rows, :], pmg_ref[...])
            row = lax.broadcasted_iota(jnp.int32, (HALO, 1), 0)
            inside = ((row < 8) & (j > 0)) | ((row >= 8) & (j < last))
            h2_new[rows, :] = jnp.where(inside, _rms(x1, pfg_ref[...]), 0.0).astype(BF16)

    proj_stage = [functools.partial(proj_piece, c) for c in range(D_MODEL // MXU_COLS)]
    norm_stage = [functools.partial(norm_piece, r) for r in range(T + 1)]

    rid = lax.broadcasted_iota(jnp.int32, (NC, 1), 0)

    def up_piece(k, part):
        c0, width = FF_BLOCKS[k]
        up = (up_a, up_b)[k % 2]
        res = _dot(h2_old[...], wup_ref[:, part * D_FF + c0:part * D_FF + c0 + width])
        nq = width // LANES
        for q in range(nq):
            up[part * nq + q] = res[:, q * LANES:(q + 1) * LANES]

    def conv_taps(up, blk, cols, t):
        cur = up[blk, t * NC:(t + 1) * NC, :]
        if t > 0:
            prev = up[blk, (t - 1) * NC:t * NC, :]
        else:
            prev = jnp.where(rid == 0, up[blk, R + 7:R + 8, :],
                             pltpu.roll(up[blk, (T - 1) * NC:T * NC, :], 1, 0))
        if t < T - 1:
            nxt = up[blk, (t + 1) * NC:(t + 2) * NC, :]
        else:
            nxt = jnp.where(rid == NC - 1, up[blk, R + 8:R + 9, :],
                            pltpu.roll(up[blk, 0:NC, :], NC - 1, 0))
        return (fcw_ref[0:1, cols] * prev + fcw_ref[1:2, cols] * cur + fcw_ref[2:3, cols] * nxt
                + fcb_ref[:, cols])

    def conv_piece(k, t):
        c0, width = FF_BLOCKS[k]
        up = (up_a, up_b)[k % 2]
        nq = width // LANES
        for q in range(nq):
            gcols = slice(c0 + q * LANES, c0 + (q + 1) * LANES)
            vcols = slice(D_FF + c0 + q * LANES, D_FF + c0 + (q + 1) * LANES)
            act = jax.nn.silu(conv_taps(up, q, gcols, t)) * conv_taps(up, nq + q, vcols, t)
            act_scr[t * NC:(t + 1) * NC, gcols] = act.astype(BF16)

    def down_piece(first, c):
        cols = slice(c * MXU_COLS, (c + 1) * MXU_COLS)
        if first:
            f_scr[:, cols] = _dot(act_scr[:, 0:FF_SPLIT], wdown_ref[0:FF_SPLIT, cols])
        else:
            f_scr[:, cols] += _dot(act_scr[:, FF_SPLIT:], wdown_ref[FF_SPLIT:, cols])

    nblk = len(FF_BLOCKS)
    up_stage = [[functools.partial(up_piece, k, part) for part in range(2)] for k in range(nblk)]
    conv_stage = [[functools.partial(conv_piece, k, t) for t in range(T)] for k in range(nblk)]
    down_stage = [[functools.partial(down_piece, first, c) for c in range(D_MODEL // MXU_COLS)]
                  for first in (True, False)]
    assert FF_SPLIT <= FF_BLOCKS[-1][0]

    _interleave(proj_stage, out_stage)
    _interleave(up_stage[0], norm_stage)
    for k in range(nblk):
        heavy = up_stage[k + 1] if k + 1 < nblk else down_stage[0]
        _interleave(heavy, conv_stage[k])
    _interleave(down_stage[1], [])


def _out_ffn(x, nconv, nssm, w_out, post_mix_g, pre_ffn_g, w_up, ffn_conv_w, ffn_conv_b,
             w_down, post_ffn_g):
    B, L, D = x.shape
    nt = L // TILE
    nj = L // FFN_TILE
    per_tile = TILE_CHUNKS // FFN_CHUNKS
    blk8 = FFN_TILE // 8
    nchunks = L // CHUNK
    nc5 = nconv.reshape(B, nt, CHUNK, TILE_CHUNKS, D_CONV)
    ns5 = nssm.reshape(B, nt, CHUNK, TILE_CHUNKS, D_SSM)

    ntiles = B * nj
    up_shape = (2 * FF_BLOCK_MAX // LANES, FFN_TILE + HALO, LANES)

    def in_tile(s):
        tile = jnp.minimum(s, ntiles - 1)
        return tile // nj, tile % nj

    def x_prev_map(s):
        b, j = in_tile(s)
        return (b, jnp.maximum(j * blk8 - 1, 0), 0)

    def x_next_map(s):
        b, j = in_tile(s)
        return (b, jnp.minimum((j + 1) * blk8, L // 8 - 1), 0)

    def main_map(s):
        b, j = in_tile(s)
        return (b, j // per_tile, 0, j % per_tile, 0)

    def prev_map(s):
        b, j = in_tile(s)
        c = jnp.maximum(j * FFN_CHUNKS - 1, 0)
        return (b, c // TILE_CHUNKS, CHUNK - 1, (c % TILE_CHUNKS) // 16, 0)

    def next_map(s):
        b, j = in_tile(s)
        c = jnp.minimum((j + 1) * FFN_CHUNKS, nchunks - 1)
        return (b, c // TILE_CHUNKS, 0, (c % TILE_CHUNKS) // 16, 0)

    act_specs = []
    for width in (D_CONV, D_SSM):
        act_specs += [pl.BlockSpec((1, 1, CHUNK, FFN_CHUNKS, width), main_map),
                      pl.BlockSpec((1, 1, 1, 16, width), prev_map),
                      pl.BlockSpec((1, 1, 1, 16, width), next_map)]
    x4 = x.reshape(B, nchunks, CHUNK, D)
    slab_shape = (CHUNK, FFN_CHUNKS, D)
    out4 = pl.pallas_call(
        functools.partial(_out_ffn_kernel, nj=nj),
        grid=(ntiles + PIPE_LAG,),
        in_specs=[
            pl.BlockSpec(memory_space=pl.ANY),
            pl.BlockSpec((1, 8, D), x_prev_map),
            pl.BlockSpec((1, 8, D), x_next_map),
        ] + act_specs + [
            _const_spec((D_CONV + D_SSM, D)),
            _const_spec((1, D)),
            _const_spec((1, D)),
            _const_spec((D, 2 * D_FF)),
            _const_spec((3, 2 * D_FF)),
            _const_spec((1, 2 * D_FF)),
            _const_spec((D_FF, D)),
            _const_spec((1, D)),
        ],
        out_specs=pl.BlockSpec(memory_space=pl.ANY),
        out_shape=jax.ShapeDtypeStruct((B, nchunks, CHUNK, D), F32),
        scratch_shapes=[pltpu.VMEM(up_shape, F32),
                        pltpu.VMEM(up_shape, F32),
                        pltpu.VMEM((FFN_TILE, D_FF), BF16),
                        pltpu.VMEM((FFN_TILE, D), F32),
                        pltpu.VMEM((FFN_TILE, D), F32),
                        pltpu.VMEM((FFN_TILE + HALO, D), BF16),
                        pltpu.VMEM((FFN_TILE + HALO, D), BF16),
                        pltpu.VMEM((FFN_TILE, D), F32),
                        pltpu.VMEM((FFN_TILE + HALO, D), F32),
                        pltpu.VMEM(slab_shape, F32),
                        pltpu.VMEM(slab_shape, F32),
                        pltpu.VMEM(slab_shape, F32),
                        pltpu.VMEM(slab_shape, F32),
                        pltpu.SemaphoreType.DMA((2, CHUNK)),
                        pltpu.SemaphoreType.DMA((2, CHUNK))],
        compiler_params=pltpu.CompilerParams(
            dimension_semantics=("arbitrary",),
            vmem_limit_bytes=VMEM_LIMIT_V7X),
        name="out_ffn",
    )(x4, x, x, nc5, nc5, nc5, ns5, ns5, ns5, w_out, post_mix_g, pre_ffn_g, w_up,
      ffn_conv_w, ffn_conv_b, w_down, post_ffn_g)
    return out4.reshape(B, L, D)


def kernel(x_prompt, x_sample, pre_mix_g, w_in, conv_w, lam_re, lam_im, log_step, b_re, b_im,
           c_re, c_im, d_skip, w_glu, b_glu, gn_conv, gn_ssm, w_out, post_mix_g,
           pre_ffn_g, w_up, ffn_conv_w, ffn_conv_b, w_down, post_ffn_g):
    assert pre_mix_g.shape[0] == 1, "one encoder layer"
    wv, zker, mst, coef = _s5_tables(lam_re[0], lam_im[0], log_step[0], b_re[0], b_im[0],
                              c_re[0], c_im[0], d_skip[0])
    w_in_b = w_in[0].astype(BF16)
    w_out_b = w_out[0].astype(BF16)
    w_up_b = w_up[0].astype(BF16)
    w_down_b = w_down[0].astype(BF16)
    wglu_t = w_glu[0].T.astype(BF16)
    bglu_col = b_glu[0].reshape(D_SSM, 1)
    gn_ssm_col = gn_ssm[0].reshape(D_SSM, 1)

    def trunk(x):
        assert x.shape[1] % TILE == 0 and x.shape[2] == D_MODEL
        nconv, zut = _mixer_in(x, pre_mix_g, w_in_b, conv_w[0], gn_conv)
        nssm = _s5_mixer(zut, wv, zker, mst, coef, wglu_t, bglu_col, gn_ssm_col)
        return _out_ffn(x, nconv, nssm, w_out_b, post_mix_g, pre_ffn_g, w_up_b,
                        ffn_conv_w[0], ffn_conv_b, w_down_b, post_ffn_g)

    return (trunk(x_prompt), trunk(x_sample))
```

```python
import functools

import jax
import jax.numpy as jnp
from jax import lax
from jax.experimental import pallas as pl
from jax.experimental.pallas import tpu as pltpu

D_MODEL = 1024
D_CONV = 512
D_SSM = 512
SSM_GROUP = 16
SSM_GROUPS = D_SSM // SSM_GROUP
SSM_STATE = 64
D_FF = 2816
EPS = 1e-6
LANES = 128

CHUNK = 16
TILE_CHUNKS = 128
TILE = CHUNK * TILE_CHUNKS
PROJ_ROWS = 512
HALO = 16
FFN_CHUNKS = 32
FFN_TILE = CHUNK * FFN_CHUNKS
MXU_COLS = 512
FF_BLOCKS = tuple((c0, min(MXU_COLS, D_FF - c0)) for c0 in range(0, D_FF, MXU_COLS))
FF_BLOCK_MAX = MXU_COLS
FF_SPLIT = FF_BLOCKS[-1][0]
PIPE_LAG = 2
SCAN_PAD = 8
PAIR_UNROLL = 4
SCAN_UNROLL = 4
GLU_COLS = 512
VMEM_LIMIT_V7X = 56 * 1024 * 1024

F32 = jnp.float32
BF16 = jnp.bfloat16


def _rms(x, g):
    return x * lax.rsqrt(jnp.mean(x * x, axis=-1, keepdims=True) + EPS) * g


def _dot(a, b):
    return jnp.dot(a, b, preferred_element_type=F32)


def _gelu_tanh(x):
    c = 0.7978845608028654
    half_x = 0.5 * x
    return half_x + half_x * jnp.tanh(x * (c + (c * 0.044715) * (x * x)))


def _interleave(heavy, light):
    i = j = 0
    while i < len(heavy) or j < len(light):
        if j >= len(light) or (i < len(heavy) and i * len(light) <= j * len(heavy)):
            heavy[i]()
            i += 1
        else:
            light[j]()
            j += 1


def _const_spec(shape):
    zeros = (0,) * len(shape)
    return pl.BlockSpec(shape, lambda *_: zeros, pipeline_mode=pl.Buffered(1))


def _s5_tables(lam_re, lam_im, log_step, b_re, b_im, c_re, c_im, d_skip):
    T, G, P, H = CHUNK, SSM_GROUPS, SSM_STATE, SSM_GROUP
    lam = lax.complex(lam_re.astype(F32), lam_im.astype(F32))
    dt = jnp.exp(log_step.astype(F32))[..., None]
    lam_bar = jnp.exp(lam * dt)
    b_bar = ((lam_bar - 1.0) / lam)[..., None] * lax.complex(b_re.astype(F32), b_im.astype(F32))
    cmat = lax.complex(c_re.astype(F32), c_im.astype(F32))
    k = jnp.arange(T + 1, dtype=F32)
    pw = jnp.exp((lam * dt)[..., None] * k)

    exact = lax.Precision.HIGHEST
    cp = cmat[:, :, None] * jnp.swapaxes(pw[..., :T], 2, 3)[:, :, :, None, :]
    kern = (jnp.einsum('dgkap,dgph->dgkah', jnp.real(cp), jnp.real(b_bar), precision=exact)
            - jnp.einsum('dgkap,dgph->dgkah', jnp.imag(cp), jnp.imag(b_bar), precision=exact))
    lag0 = kern[0][:, 0] + kern[1][:, 0] + jnp.eye(H, dtype=F32) * d_skip.astype(F32).reshape(G, 1, H)
    lags = jnp.concatenate([kern[0][:, :0:-1], lag0[:, None], kern[1][:, 1:],
                            jnp.zeros((G, 1, H, H), F32)], axis=1)
    zker = lags.transpose(0, 2, 1, 3).reshape(G * H, 2 * T * H)

    inc_f = pw[0][:, :, T - 1::-1][..., None] * b_bar[0][:, :, None, :]
    inc_b = pw[1][:, :, :T][..., None] * b_bar[1][:, :, None, :]
    wv = jnp.concatenate([jnp.real(inc_f), jnp.imag(inc_f), jnp.real(inc_b), jnp.imag(inc_b)],
                         axis=1).reshape(G, 4 * P, T * H)

    st_f = cmat[0][:, None] * jnp.swapaxes(pw[0][:, :, 1:], 1, 2)[:, :, None, :]
    st_b = cmat[1][:, None] * jnp.swapaxes(pw[1][:, :, T:0:-1], 1, 2)[:, :, None, :]
    mst = jnp.concatenate([jnp.real(st_f), -jnp.imag(st_f), jnp.real(st_b), -jnp.imag(st_b)],
                          axis=-1).reshape(G, T * H, 4 * P)

    a16 = pw[..., T].reshape(2, G // 2, 2 * P)
    coef = jnp.stack([jnp.real(a16[0]), jnp.imag(a16[0]), jnp.real(a16[1]), jnp.imag(a16[1])])
    return wv.astype(BF16), zker, mst.astype(BF16), coef.astype(F32)


def _slab_copies(hbm4, b, c0, n_chunks, slabs, sems, to_hbm):
    copies = []
    for t in range(CHUNK):
        window = hbm4.at[b, pl.ds(c0, n_chunks), t]
        src, dst = (slabs.at[t], window) if to_hbm else (window, slabs.at[t])
        copies.append(pltpu.make_async_copy(src, dst, sems.at[t]))
    return copies


def _mixer_in_kernel(x4_ref, xp_ref, xn_ref, g_ref, win_ref, cw_ref, gn_ref,
                     nconv_ref, zut_ref, hn_even, hn_odd, xs_even, xs_odd, p_scr, zb_scr,
                     x_sems, *, nt):
    s = pl.program_id(0)
    nsteps = pl.num_programs(0)
    ntiles = nsteps - 1

    def x_copies(tile, slabs, sems):
        return _slab_copies(x4_ref, tile // nt, (tile % nt) * TILE_CHUNKS, TILE_CHUNKS,
                            slabs, sems, to_hbm=False)

    @pl.when(s == 0)
    def _():
        for ref in (hn_even, hn_odd):
            ref[...] = jnp.zeros(ref.shape, ref.dtype)
        for copy in x_copies(0, xs_even, x_sems.at[0]):
            copy.start()

    hns, xss = (hn_even, hn_odd), (xs_even, xs_odd)
    for parity in range(2):
        other = 1 - parity

        @pl.when(s % 2 == parity)
        def _(parity=parity, other=other):
            tile = jnp.minimum(s, ntiles - 1)
            for copy in x_copies(tile, xss[parity], x_sems.at[parity]):
                copy.wait()
            for copy in x_copies(jnp.minimum(s + 1, ntiles - 1), xss[other], x_sems.at[other]):
                copy.start()
            _mixer_in_step(xss[parity], xp_ref, xn_ref, g_ref, win_ref, cw_ref, gn_ref,
                           nconv_ref, zut_ref, hns[parity], hns[other], p_scr, zb_scr,
                           tile % nt, nt)

    @pl.when(s == nsteps - 1)
    def _():
        nxt = 1 - (nsteps - 1) % 2
        for copy in x_copies(ntiles - 1, xss[nxt], x_sems.at[nxt]):
            copy.wait()


def _mixer_in_step(x_slabs, xp_ref, xn_ref, g_ref, win_ref, cw_ref, gn_ref, nconv_ref, zut_ref,
                   hn_new, hn_old, p_scr, zb_scr, i, nt):
    T, NC = CHUNK, TILE_CHUNKS
    half = D_CONV // 2

    def norm_piece(t):
        if t < T:
            hn_new[t * NC:(t + 1) * NC, :] = _rms(x_slabs[t], g_ref[...]).astype(BF16)
        else:
            xh = jnp.concatenate([xp_ref[0], xn_ref[0]], axis=0)
            row = lax.broadcasted_iota(jnp.int32, (HALO, 1), 0)
            inside = ((row < 8) & (i > 0)) | ((row >= 8) & (i < nt - 1))
            hn_new[TILE:TILE + HALO, :] = jnp.where(inside, _rms(xh, g_ref[...]), 0.0).astype(BF16)

    norm_stage = [functools.partial(norm_piece, t) for t in range(T + 1)]

    nblk = TILE // PROJ_ROWS

    def block_rows(r, with_halo):
        end = TILE + HALO if (with_halo and r == nblk - 1) else (r + 1) * PROJ_ROWS
        return slice(r * PROJ_ROWS, end)

    def gate_piece(r, k):
        rows = block_rows(r, True)
        zc = _dot(hn_old[rows, :], win_ref[:, D_CONV + k * half:D_CONV + (k + 1) * half])
        zx = _dot(hn_old[rows, :], win_ref[:, 2 * D_CONV + k * half:2 * D_CONV + (k + 1) * half])
        p_scr[rows, k * half:(k + 1) * half] = zc * zx

    def zb_piece(r):
        rows = block_rows(r, False)
        zb_scr[rows, :] = _dot(hn_old[rows, :], win_ref[:, 0:D_CONV])

    def zu_piece(r):
        rows = block_rows(r, False)
        zu = _dot(hn_new[rows, :], win_ref[:, 3 * D_CONV:])
        zut_ref[0, 0, :, rows] = zu.T.astype(BF16)

    rid = lax.broadcasted_iota(jnp.int32, (NC, 1), 0)

    def conv_piece(t):
        cur = p_scr[t * NC:(t + 1) * NC, :]
        if t > 0:
            prev = p_scr[(t - 1) * NC:t * NC, :]
        else:
            prev = jnp.where(rid == 0, p_scr[TILE + 7:TILE + 8, :],
                             pltpu.roll(p_scr[(T - 1) * NC:T * NC, :], 1, 0))
        if t < T - 1:
            nxt = p_scr[(t + 1) * NC:(t + 2) * NC, :]
        else:
            nxt = jnp.where(rid == NC - 1, p_scr[TILE + 8:TILE + 9, :],
                            pltpu.roll(p_scr[0:NC, :], NC - 1, 0))
        yc = zb_scr[t * NC:(t + 1) * NC, :] * (cw_ref[0:1, :] * prev + cw_ref[1:2, :] * cur
                                               + cw_ref[2:3, :] * nxt)
        nconv_ref[0, 0, t * NC:(t + 1) * NC, :] = _rms(yc, gn_ref[...]).astype(BF16)

    gate_stage = [functools.partial(gate_piece, r, k) for r in range(nblk) for k in range(2)]
    zb_stage = [functools.partial(zb_piece, r) for r in range(nblk)]
    zu_stage = [functools.partial(zu_piece, r) for r in range(nblk)]
    conv_stage = [functools.partial(conv_piece, t) for t in range(T)]
    _interleave(gate_stage + zb_stage, norm_stage)
    _interleave(zu_stage, conv_stage)


def _mixer_in(x, pre_mix_g, w_in, conv_w, gn_conv):
    B, L, D = x.shape
    nt = L // TILE
    blk8 = TILE // 8
    ntiles = B * nt

    def in_tile(s):
        tile = jnp.minimum(s, ntiles - 1)
        return tile // nt, tile % nt

    def out_map(s):
        tile = jnp.maximum(s - 1, 0)
        return (tile // nt, tile % nt, 0, 0)

    def x_prev_map(s):
        b, i = in_tile(s)
        return (b, jnp.maximum(i * blk8 - 1, 0), 0)

    def x_next_map(s):
        b, i = in_tile(s)
        return (b, jnp.minimum((i + 1) * blk8, L // 8 - 1), 0)

    x4 = x.reshape(B, L // CHUNK, CHUNK, D)
    slab_shape = (CHUNK, TILE_CHUNKS, D)
    return pl.pallas_call(
        functools.partial(_mixer_in_kernel, nt=nt),
        grid=(ntiles + 1,),
        in_specs=[
            pl.BlockSpec(memory_space=pl.ANY),
            pl.BlockSpec((1, 8, D), x_prev_map),
            pl.BlockSpec((1, 8, D), x_next_map),
            _const_spec((1, D)),
            _const_spec((D, 4 * D_CONV)),
            _const_spec((3, D_CONV)),
            _const_spec((1, D_CONV)),
        ],
        out_specs=[
            pl.BlockSpec((1, 1, TILE, D_CONV), out_map),
            pl.BlockSpec((1, 1, D_SSM, TILE), lambda s: in_tile(s) + (0, 0)),
        ],
        out_shape=[
            jax.ShapeDtypeStruct((B, nt, TILE, D_CONV), BF16),
            jax.ShapeDtypeStruct((B, nt, D_SSM, TILE), BF16),
        ],
        scratch_shapes=[pltpu.VMEM((TILE + HALO, D), BF16),
                        pltpu.VMEM((TILE + HALO, D), BF16),
                        pltpu.VMEM(slab_shape, F32),
                        pltpu.VMEM(slab_shape, F32),
                        pltpu.VMEM((TILE + HALO, D_CONV), F32),
                        pltpu.VMEM((TILE, D_CONV), F32),
                        pltpu.SemaphoreType.DMA((2, CHUNK))],
        compiler_params=pltpu.CompilerParams(
            dimension_semantics=("arbitrary",),
            vmem_limit_bytes=VMEM_LIMIT_V7X),
        name="mixer_in",
    )(x4, x, x, pre_mix_g, w_in, conv_w, gn_conv)


def _s5_kernel(zut_ref, wv_ref, zker_ref, mst_ref, coef_ref, wglu_ref, bglu_ref, gn_ref,
               out_ref, vfr, vfi, vbr, vbi, yt_scr, toe_ref, *, nt):
    T, NC, H, P = CHUNK, TILE_CHUNKS, SSM_GROUP, SSM_STATE
    npair = SSM_GROUPS // 2
    nc = nt * NC
    pitch = nc + SCAN_PAD

    @pl.when(pl.program_id(0) == 0)
    def _():
        width = 2 * T * H

        pltpu.sync_copy(zker_ref, yt_scr.at[0, :, 0:width])

        def expand(g, carry):
            z = yt_scr[0, pl.ds(pl.multiple_of(g * H, H), H), 0:width]
            for t in range(T):
                off = (T - 1 - t) * H
                win = pltpu.roll(z, (width - off) % width, 1) if off else z
                toe_ref[g, t * H:(t + 1) * H, :] = win[:, 0:T * H].astype(BF16)
            return carry
        lax.fori_loop(0, SSM_GROUPS, expand, 0)

    def chunk_inputs(grp):
        r0 = pl.multiple_of(grp * H, H)
        return jnp.concatenate(
            [jnp.concatenate([zut_ref[0, i, pl.ds(r0, H), t * NC:(t + 1) * NC] for t in range(T)],
                             axis=0) for i in range(nt)], axis=1)

    def inc_body(q, carry):
        va = _dot(wv_ref[2 * q], chunk_inputs(2 * q))
        vb = _dot(wv_ref[2 * q + 1], chunk_inputs(2 * q + 1))
        pieces = []
        for kind in range(4):
            pieces += [va[kind * P:(kind + 1) * P], vb[kind * P:(kind + 1) * P]]
        v = jnp.concatenate(pieces, axis=0).T
        r0 = pl.multiple_of(q * pitch, 8)
        vfr[pl.ds(r0, nc), :] = v[:, 0:2 * P]
        vfi[pl.ds(r0, nc), :] = v[:, 2 * P:4 * P]
        vbr[pl.ds(r0, nc), :] = v[:, 4 * P:6 * P]
        vbi[pl.ds(r0, nc), :] = v[:, 6 * P:8 * P]
        return carry
    lax.fori_loop(0, npair, inc_body, 0, unroll=PAIR_UNROLL)

    afr, afi, abr, abi = coef_ref[0], coef_ref[1], coef_ref[2], coef_ref[3]

    def scan_body(k, carry):
        sfr, sfi, sbr, sbi = carry
        rows_f = pl.ds(k, npair, stride=pitch)
        rows_b = pl.ds(nc - 1 - k, npair, stride=pitch)
        ur, ui = vfr[rows_f, :], vfi[rows_f, :]
        wr, wi = vbr[rows_b, :], vbi[rows_b, :]
        vfr[rows_f, :] = sfr
        vfi[rows_f, :] = sfi
        vbr[rows_b, :] = sbr
        vbi[rows_b, :] = sbi
        return (afr * sfr - afi * sfi + ur, afr * sfi + afi * sfr + ui,
                abr * sbr - abi * sbi + wr, abr * sbi + abi * sbr + wi)

    zero = jnp.zeros((npair, 2 * P), F32)
    lax.fori_loop(0, nc, scan_body, (zero, zero, zero, zero), unroll=SCAN_UNROLL)

    def out_body(q, carry):
        rows = pl.ds(pl.multiple_of(q * pitch, 8), nc)
        sin = jnp.concatenate([vfr[rows, :], vfi[rows, :], vbr[rows, :], vbi[rows, :]],
                              axis=1).T
        for j in range(2):
            grp = 2 * q + j
            st = jnp.concatenate([sin[(2 * kind + j) * P:(2 * kind + j + 1) * P]
                                  for kind in range(4)], axis=0)
            yt = (_dot(toe_ref[grp], chunk_inputs(grp))
                  + _dot(mst_ref[grp], st.astype(BF16)))
            h0 = pl.multiple_of(grp * H, H)
            for i in range(nt):
                for t in range(T):
                    yt_scr[i, pl.ds(h0, H), t * NC:(t + 1) * NC] = yt[t * H:(t + 1) * H,
                                                                      i * NC:(i + 1) * NC]
        return carry
    lax.fori_loop(0, npair, out_body, 0, unroll=PAIR_UNROLL)

    for i in range(nt):
        for cb in range(TILE // GLU_COLS):
            cols = slice(cb * GLU_COLS, (cb + 1) * GLU_COLS)
            y = _gelu_tanh(yt_scr[i, :, cols])
            gate = _dot(wglu_ref[...], y.astype(BF16)) + bglu_ref[...]
            y = y * jax.nn.sigmoid(gate)
            ms = jnp.mean(y * y, axis=0, keepdims=True)
            y = y * lax.rsqrt(ms + EPS) * gn_ref[...]
            out_ref[0, i, cols, :] = y.T.astype(BF16)


def _s5_mixer(zut, wv, zker, mst, coef, wglu_t, bglu_col, gn_col):
    B, nt = zut.shape[0], zut.shape[1]
    npair = SSM_GROUPS // 2
    scan_rows = npair * (nt * TILE_CHUNKS + SCAN_PAD)
    return pl.pallas_call(
        functools.partial(_s5_kernel, nt=nt),
        grid=(B,),
        in_specs=[
            pl.BlockSpec((1, nt, D_SSM, TILE), lambda b: (b, 0, 0, 0)),
            _const_spec(wv.shape),
            pl.BlockSpec(memory_space=pl.ANY),
            _const_spec(mst.shape),
            _const_spec(coef.shape),
            _const_spec((D_SSM, D_SSM)),
            _const_spec((D_SSM, 1)),
            _const_spec((D_SSM, 1)),
        ],
        out_specs=pl.BlockSpec((1, nt, TILE, D_SSM), lambda b: (b, 0, 0, 0)),
        out_shape=jax.ShapeDtypeStruct((B, nt, TILE, D_SSM), BF16),
        scratch_shapes=[pltpu.VMEM((scan_rows, 2 * SSM_STATE), F32) for _ in range(4)]
        + [pltpu.VMEM((nt, D_SSM, TILE), F32),
           pltpu.VMEM((SSM_GROUPS, CHUNK * SSM_GROUP, CHUNK * SSM_GROUP), BF16)],
        compiler_params=pltpu.CompilerParams(
            dimension_semantics=("arbitrary",),
            vmem_limit_bytes=VMEM_LIMIT_V7X),
        name="s5_mixer",
    )(zut, wv, zker, mst, coef, wglu_t, bglu_col, gn_col)


def _ffn_slab_copies(hbm4, tile, nj, slabs, sems, to_hbm):
    return _slab_copies(hbm4, tile // nj, (tile % nj) * FFN_CHUNKS, FFN_CHUNKS, slabs, sems, to_hbm)


def _out_ffn_kernel(x4_ref, xp_ref, xn_ref, nc_ref, ncp_ref, ncn_ref, ns_ref, nsp_ref, nsn_ref,
                    wout_ref, pmg_ref, pfg_ref, wup_ref, fcw_ref, fcb_ref, wdown_ref, pog_ref,
                    o4_ref, up_a, up_b, act_scr, x1_even, x1_odd, h2_even, h2_odd, f_scr,
                    mix_scr, xs_even, xs_odd, os_even, os_odd, x_sems, o_sems, *, nj):
    s = pl.program_id(0)
    nsteps = pl.num_programs(0)
    ntiles = nsteps - PIPE_LAG

    @pl.when(s == 0)
    def _():
        for ref in (x1_even, x1_odd, h2_even, h2_odd, f_scr):
            ref[...] = jnp.zeros(ref.shape, ref.dtype)
        for copy in _ffn_slab_copies(x4_ref, 0, nj, xs_even, x_sems.at[0], to_hbm=False):
            copy.start()

    shared = (xp_ref, xn_ref, nc_ref, ncp_ref, ncn_ref, ns_ref, nsp_ref, nsn_ref,
              wout_ref, pmg_ref, pfg_ref, wup_ref, fcw_ref, fcb_ref, wdown_ref, pog_ref,
              up_a, up_b, act_scr, f_scr, mix_scr)
    x1s, h2s = (x1_even, x1_odd), (h2_even, h2_odd)
    xss, oss = (xs_even, xs_odd), (os_even, os_odd)
    for parity in range(2):
        other = 1 - parity

        @pl.when(s % 2 == parity)
        def _(parity=parity, other=other):
            x_tile = jnp.minimum(s, ntiles - 1)
            for copy in _ffn_slab_copies(x4_ref, x_tile, nj, xss[parity], x_sems.at[parity], False):
                copy.wait()
            x_next = jnp.minimum(s + 1, ntiles - 1)
            for copy in _ffn_slab_copies(x4_ref, x_next, nj, xss[other], x_sems.at[other], False):
                copy.start()
            o_tile = jnp.where(s >= PIPE_LAG, s - PIPE_LAG, s)
            o_prev = jnp.where(s >= 2 * PIPE_LAG, s - 2 * PIPE_LAG, s - PIPE_LAG)

            @pl.when(s >= PIPE_LAG)
            def _():
                for copy in _ffn_slab_copies(o4_ref, o_prev, nj, oss[parity], o_sems.at[parity], True):
                    copy.wait()

            _out_ffn_step(shared, xss[parity], oss[parity], x1s[parity], h2s[parity], h2s[other],
                          s, ntiles, nj)
            for copy in _ffn_slab_copies(o4_ref, o_tile, nj, oss[parity], o_sems.at[parity], True):
                copy.start()

    @pl.when(s == nsteps - 1)
    def _():
        last_parity = (ntiles + PIPE_LAG - 1) % 2
        nxt = 1 - last_parity
        for copy in _ffn_slab_copies(x4_ref, ntiles - 1, nj, xss[nxt], x_sems.at[nxt], False):
            copy.wait()
        for par, tile in ((nxt, ntiles - 2), (last_parity, ntiles - 1)):
            for copy in _ffn_slab_copies(o4_ref, tile, nj, oss[par], o_sems.at[par], True):
                copy.wait()


def _out_ffn_step(shared, x_slabs, o_slabs, x1_tile, h2_new, h2_old, s, ntiles, nj):
    (xp_ref, xn_ref, nc_ref, ncp_ref, ncn_ref, ns_ref, nsp_ref, nsn_ref,
     wout_ref, pmg_ref, pfg_ref, wup_ref, fcw_ref, fcb_ref, wdown_ref, pog_ref,
     up_a, up_b, act_scr, f_scr, mix_scr) = shared
    j = jnp.minimum(s, ntiles - 1) % nj
    last = nj - 1
    T, NC, R = CHUNK, FFN_CHUNKS, FFN_TILE


    f_prev = f_scr[...]
    f_scale = lax.rsqrt(jnp.mean(f_prev * f_prev, axis=-1, keepdims=True) + EPS)

    def out_piece(t):
        rows = slice(t * NC, (t + 1) * NC)
        o_slabs[t] = x1_tile[rows, :] + f_scr[rows, :] * f_scale[rows] * pog_ref[...]

    out_stage = [functools.partial(out_piece, t) for t in range(T)]

    def with_halo(main_ref, prev_ref, next_ref):
        main = main_ref[0, 0].reshape(R, main_ref.shape[-1])
        halo = jnp.concatenate([prev_ref[0, 0, 0].astype(F32)[8:16],
                                next_ref[0, 0, 0].astype(F32)[0:8]], axis=0).astype(BF16)
        return jnp.concatenate([main, halo], axis=0)

    def proj_piece(c):
        cols = slice(c * MXU_COLS, (c + 1) * MXU_COLS)
        lhs = jnp.concatenate([with_halo(nc_ref, ncp_ref, ncn_ref),
                               with_halo(ns_ref, nsp_ref, nsn_ref)], axis=1)
        mix_scr[:, cols] = _dot(lhs, wout_ref[:, cols])

    def norm_piece(r):
        if r < T:
            rows = slice(r * NC, (r + 1) * NC)
            x1 = x_slabs[r] + _rms(mix_scr[rows, :], pmg_ref[...])
            x1_tile[rows, :] = x1
            h2_new[rows, :] = _rms(x1, pfg_ref[...]).astype(BF16)
        else:
            rows = slice(R, R + HALO)
            xr = jnp.concatenate([xp_ref[0], xn_ref[0]], axis=0)
            x1 = xr + _rms(mix_scr[rows, :], pmg_ref[...])
            row = lax.broadcasted_iota(jnp.int32, (HALO, 1), 0)
            inside = ((row < 8) & (j > 0)) | ((row >= 8) & (j < last))
            h2_new[rows, :] = jnp.where(inside, _rms(x1, pfg_ref[...]), 0.0).astype(BF16)

    proj_stage = [functools.partial(proj_piece, c) for c in range(D_MODEL // MXU_COLS)]
    norm_stage = [functools.partial(norm_piece, r) for r in range(T + 1)]

    rid = lax.broadcasted_iota(jnp.int32, (NC, 1), 0)

    def up_piece(k, part):
        c0, width = FF_BLOCKS[k]
        up = (up_a, up_b)[k % 2]
        res = _dot(h2_old[...], wup_ref[:, part * D_FF + c0:part * D_FF + c0 + width])
        nq = width // LANES
        for q in range(nq):
            up[part * nq + q] = res[:, q * LANES:(q + 1) * LANES]

    def conv_taps(up, blk, cols, t):
        cur = up[blk, t * NC:(t + 1) * NC, :]
        if t > 0:
            prev = up[blk, (t - 1) * NC:t * NC, :]
        else:
            prev = jnp.where(rid == 0, up[blk, R + 7:R + 8, :],
                             pltpu.roll(up[blk, (T - 1) * NC:T * NC, :], 1, 0))
        if t < T - 1:
            nxt = up[blk, (t + 1) * NC:(t + 2) * NC, :]
        else:
            nxt = jnp.where(rid == NC - 1, up[blk, R + 8:R + 9, :],
                            pltpu.roll(up[blk, 0:NC, :], NC - 1, 0))
        return (fcw_ref[0:1, cols] * prev + fcw_ref[1:2, cols] * cur + fcw_ref[2:3, cols] * nxt
                + fcb_ref[:, cols])

    def conv_piece(k, t):
        c0, width = FF_BLOCKS[k]
        up = (up_a, up_b)[k % 2]
        nq = width // LANES
        for q in range(nq):
            gcols = slice(c0 + q * LANES, c0 + (q + 1) * LANES)
            vcols = slice(D_FF + c0 + q * LANES, D_FF + c0 + (q + 1) * LANES)
            act = jax.nn.silu(conv_taps(up, q, gcols, t)) * conv_taps(up, nq + q, vcols, t)
            act_scr[t * NC:(t + 1) * NC, gcols] = act.astype(BF16)

    def down_piece(first, c):
        cols = slice(c * MXU_COLS, (c + 1) * MXU_COLS)
        if first:
            f_scr[:, cols] = _dot(act_scr[:, 0:FF_SPLIT], wdown_ref[0:FF_SPLIT, cols])
        else:
            f_scr[:, cols] += _dot(act_scr[:, FF_SPLIT:], wdown_ref[FF_SPLIT:, cols])

    nblk = len(FF_BLOCKS)
    up_stage = [[functools.partial(up_piece, k, part) for part in range(2)] for k in range(nblk)]
    conv_stage = [[functools.partial(conv_piece, k, t) for t in range(T)] for k in range(nblk)]
    down_stage = [[functools.partial(down_piece, first, c) for c in range(D_MODEL // MXU_COLS)]
                  for first in (True, False)]
    assert FF_SPLIT <= FF_BLOCKS[-1][0]

    _interleave(proj_stage, out_stage)
    _interleave(up_stage[0], norm_stage)
    for k in range(nblk):
        heavy = up_stage[k + 1] if k + 1 < nblk else down_stage[0]
        _interleave(heavy, conv_stage[k])
    _interleave(down_stage[1], [])


def _out_ffn(x, nconv, nssm, w_out, post_mix_g, pre_ffn_g, w_up, ffn_conv_w, ffn_conv_b,
             w_down, post_ffn_g):
    B, L, D = x.shape
    nt = L // TILE
    nj = L // FFN_TILE
    per_tile = TILE_CHUNKS // FFN_CHUNKS
    blk8 = FFN_TILE // 8
    nchunks = L // CHUNK
    nc5 = nconv.reshape(B, nt, CHUNK, TILE_CHUNKS, D_CONV)
    ns5 = nssm.reshape(B, nt, CHUNK, TILE_CHUNKS, D_SSM)

    ntiles = B * nj
    up_shape = (2 * FF_BLOCK_MAX // LANES, FFN_TILE + HALO, LANES)

    def in_tile(s):
        tile = jnp.minimum(s, ntiles - 1)
        return tile // nj, tile % nj

    def x_prev_map(s):
        b, j = in_tile(s)
        return (b, jnp.maximum(j * blk8 - 1, 0), 0)

    def x_next_map(s):
        b, j = in_tile(s)
        return (b, jnp.minimum((j + 1) * blk8, L // 8 - 1), 0)

    def main_map(s):
        b, j = in_tile(s)
        return (b, j // per_tile, 0, j % per_tile, 0)

    def prev_map(s):
        b, j = in_tile(s)
        c = jnp.maximum(j * FFN_CHUNKS - 1, 0)
        return (b, c // TILE_CHUNKS, CHUNK - 1, (c % TILE_CHUNKS) // 16, 0)

    def next_map(s):
        b, j = in_tile(s)
        c = jnp.minimum((j + 1) * FFN_CHUNKS, nchunks - 1)
        return (b, c // TILE_CHUNKS, 0, (c % TILE_CHUNKS) // 16, 0)

    act_specs = []
    for width in (D_CONV, D_SSM):
        act_specs += [pl.BlockSpec((1, 1, CHUNK, FFN_CHUNKS, width), main_map),
                      pl.BlockSpec((1, 1, 1, 16, width), prev_map),
                      pl.BlockSpec((1, 1, 1, 16, width), next_map)]
    x4 = x.reshape(B, nchunks, CHUNK, D)
    slab_shape = (CHUNK, FFN_CHUNKS, D)
    out4 = pl.pallas_call(
        functools.partial(_out_ffn_kernel, nj=nj),
        grid=(ntiles + PIPE_LAG,),
        in_specs=[
            pl.BlockSpec(memory_space=pl.ANY),
            pl.BlockSpec((1, 8, D), x_prev_map),
            pl.BlockSpec((1, 8, D), x_next_map),
        ] + act_specs + [
            _const_spec((D_CONV + D_SSM, D)),
            _const_spec((1, D)),
            _const_spec((1, D)),
            _const_spec((D, 2 * D_FF)),
            _const_spec((3, 2 * D_FF)),
            _const_spec((1, 2 * D_FF)),
            _const_spec((D_FF, D)),
            _const_spec((1, D)),
        ],
        out_specs=pl.BlockSpec(memory_space=pl.ANY),
        out_shape=jax.ShapeDtypeStruct((B, nchunks, CHUNK, D), F32),
        scratch_shapes=[pltpu.VMEM(up_shape, F32),
                        pltpu.VMEM(up_shape, F32),
                        pltpu.VMEM((FFN_TILE, D_FF), BF16),
                        pltpu.VMEM((FFN_TILE, D), F32),
                        pltpu.VMEM((FFN_TILE, D), F32),
                        pltpu.VMEM((FFN_TILE + HALO, D), BF16),
                        pltpu.VMEM((FFN_TILE + HALO, D), BF16),
                        pltpu.VMEM((FFN_TILE, D), F32),
                        pltpu.VMEM((FFN_TILE + HALO, D), F32),
                        pltpu.VMEM(slab_shape, F32),
                        pltpu.VMEM(slab_shape, F32),
                        pltpu.VMEM(slab_shape, F32),
                        pltpu.VMEM(slab_shape, F32),
                        pltpu.SemaphoreType.DMA((2, CHUNK)),
                        pltpu.SemaphoreType.DMA((2, CHUNK))],
        compiler_params=pltpu.CompilerParams(
            dimension_semantics=("arbitrary",),
            vmem_limit_bytes=VMEM_LIMIT_V7X),
        name="out_ffn",
    )(x4, x, x, nc5, nc5, nc5, ns5, ns5, ns5, w_out, post_mix_g, pre_ffn_g, w_up,
      ffn_conv_w, ffn_conv_b, w_down, post_ffn_g)
    return out4.reshape(B, L, D)


def kernel(x_prompt, x_sample, pre_mix_g, w_in, conv_w, lam_re, lam_im, log_step, b_re, b_im,
           c_re, c_im, d_skip, w_glu, b_glu, gn_conv, gn_ssm, w_out, post_mix_g,
           pre_ffn_g, w_up, ffn_conv_w, ffn_conv_b, w_down, post_ffn_g):
    assert pre_mix_g.shape[0] == 1, "one encoder layer"
    wv, zker, mst, coef = _s5_tables(lam_re[0], lam_im[0], log_step[0], b_re[0], b_im[0],
                              c_re[0], c_im[0], d_skip[0])
    w_in_b = w_in[0].astype(BF16)
    w_out_b = w_out[0].astype(BF16)
    w_up_b = w_up[0].astype(BF16)
    w_down_b = w_down[0].astype(BF16)
    wglu_t = w_glu[0].T.astype(BF16)
    bglu_col = b_glu[0].reshape(D_SSM, 1)
    gn_ssm_col = gn_ssm[0].reshape(D_SSM, 1)

    def trunk(x):
        assert x.shape[1] % TILE == 0 and x.shape[2] == D_MODEL
        nconv, zut = _mixer_in(x, pre_mix_g, w_in_b, conv_w[0], gn_conv)
        nssm = _s5_mixer(zut, wv, zker, mst, coef, wglu_t, bglu_col, gn_ssm_col)
        return _out_ffn(x, nconv, nssm, w_out_b, post_mix_g, pre_ffn_g, w_up_b,
                        ffn_conv_w[0], ffn_conv_b, w_down_b, post_ffn_g)

    return (trunk(x_prompt), trunk(x_sample))
```
